```python
import math
import jax, jax.numpy as jnp
from jax import lax
import numpy as np

D_MODEL = 2048
BATCH = 2
SEQ = 8192
DEPTH = 4

N_MIXERS = 2
MEM_TOKENS = 256
MIX_WIDTH = 2 * D_MODEL
MEM_HEADS = 4
MEM_WIDTH = MIX_WIDTH // 4
MEM_HEAD_DIM = MEM_WIDTH // MEM_HEADS
MAIN_WIDTH = MIX_WIDTH - MEM_WIDTH
SSD_HEAD_DIM = 64
SSD_HEADS = MAIN_WIDTH // SSD_HEAD_DIM
SSD_GROUPS = 8
SSD_STATE = 128
SSD_CONV = 4
SSD_CHUNK = 128
SSD_CONV_DIM = MAIN_WIDTH + 2 * SSD_GROUPS * SSD_STATE
SB_HEAD_DIM = 128
SB_HEADS = MAIN_WIDTH // SB_HEAD_DIM
SB_BLOCK = 128
EPS = 1e-6

kernel_name = "hybrid_ssd_stickbreaking_memory_trunk"


def rmsnorm(x, g):
    xf = x.astype(jnp.float32)
    y = xf * lax.rsqrt(jnp.mean(xf * xf, axis=-1, keepdims=True) + EPS)
    return (y * g.astype(jnp.float32)).astype(x.dtype)


def causal_depthwise_conv(u, w, bias):
    k, c = w.shape
    out = lax.conv_general_dilated(u, w[:, None, :], window_strides=(1,), padding=[(k - 1, 0)],
                                   dimension_numbers=("NWC", "WIO", "NWC"), feature_group_count=c)
    return out + bias


def ssd_chunked(x, dt, a, bm, cm):
    b, l, h, p = x.shape
    g, n = bm.shape[2], bm.shape[3]
    r = h // g
    c = l // SSD_CHUNK
    q = SSD_CHUNK
    x = x.reshape(b, c, q, g, r, p)
    dt = dt.reshape(b, c, q, g, r)
    bm = bm.reshape(b, c, q, g, n)
    cm = cm.reshape(b, c, q, g, n)
    log_decay = jnp.cumsum(dt * a.reshape(g, r), axis=2)
    cum = jnp.moveaxis(log_decay, 2, -1)
    seg = cum[..., :, None] - cum[..., None, :]
    causal = jnp.tril(jnp.ones((q, q), dtype=bool))
    decay = jnp.exp(jnp.where(causal, seg, -jnp.inf))
    cb = jnp.einsum("bctgn,bcsgn->bcgts", cm, bm)
    w = cb[:, :, :, None] * decay * jnp.moveaxis(dt, 2, -1)[..., None, :]
    y_diag = jnp.einsum("bcgrts,bcsgrp->bctgrp", w, x)
    to_end = jnp.exp(log_decay[:, :, -1:] - log_decay) * dt
    chunk_states = jnp.einsum("bcsgn,bcsgr,bcsgrp->bcgrpn", bm, to_end, x)
    chunk_decay = jnp.exp(log_decay[:, :, -1])

    def step(state, inp):
        s_c, d_c = inp
        return state * d_c[..., None, None] + s_c, state

    init = jnp.zeros((b, g, r, p, n), jnp.float32)
    _, entering = lax.scan(step, init, (jnp.moveaxis(chunk_states, 1, 0), jnp.moveaxis(chunk_decay, 1, 0)))
    entering = jnp.moveaxis(entering, 0, 1)
    y_off = jnp.einsum("bctgn,bcgrpn,bctgr->bctgrp", cm, entering, jnp.exp(log_decay))
    return (y_diag + y_off).reshape(b, l, h, p)


def ssd_branch(h, w_in, conv_w, conv_b, dt_bias, a_log, d_skip):
    b, l, _ = h.shape
    proj = h @ w_in
    xbc, dt_raw, z, q_mem = jnp.split(
        proj, [SSD_CONV_DIM, SSD_CONV_DIM + SSD_HEADS, SSD_CONV_DIM + SSD_HEADS + MIX_WIDTH], axis=-1)
    xbc = jax.nn.silu(causal_depthwise_conv(xbc, conv_w, conv_b))
    xs, bm, cm = jnp.split(xbc, [MAIN_WIDTH, MAIN_WIDTH + SSD_GROUPS * SSD_STATE], axis=-1)
    xs = xs.reshape(b, l, SSD_HEADS, SSD_HEAD_DIM).astype(jnp.float32)
    dt = jax.nn.softplus(dt_raw.astype(jnp.float32) + dt_bias.astype(jnp.float32))
    a = -jnp.exp(a_log.astype(jnp.float32))
    y = ssd_chunked(xs, dt, a,
                    bm.reshape(b, l, SSD_GROUPS, SSD_STATE).astype(jnp.float32),
                    cm.reshape(b, l, SSD_GROUPS, SSD_STATE).astype(jnp.float32))
    y = y + xs * d_skip.astype(jnp.float32)[:, None]
    return y.reshape(b, l, MAIN_WIDTH).astype(h.dtype), z, q_mem


def stick_breaking_attention(q, k, v):
    b, h, l, d = q.shape
    nb = l // SB_BLOCK
    scale = d ** -0.5
    qb = q.astype(jnp.float32).reshape(b, h, nb, SB_BLOCK, d)
    kb = k.astype(jnp.float32).reshape(b, h, nb, SB_BLOCK, d)
    vb = v.astype(jnp.float32).reshape(b, h, nb, SB_BLOCK, d)
    strict = jnp.arange(SB_BLOCK)[:, None] > jnp.arange(SB_BLOCK)[None, :]

    def query_block(i):
        qi = lax.dynamic_index_in_dim(qb, i, axis=2, keepdims=False)

        def body(step, carry):
            o, acc = carry
            j = i - step
            kj = lax.dynamic_index_in_dim(kb, j, axis=2, keepdims=False)
            vj = lax.dynamic_index_in_dim(vb, j, axis=2, keepdims=False)
            z = jnp.einsum("bhqd,bhkd->bhqk", qi, kj) * scale
            allowed = jnp.where(j == i, strict, True)
            log_keep = jnp.where(allowed, -jax.nn.softplus(z), 0.0)
            after = lax.cumsum(log_keep, axis=3, reverse=True) - log_keep
            log_a = jax.nn.log_sigmoid(z) + after + acc[..., None]
            a = jnp.where(allowed, jnp.exp(log_a), 0.0)
            o = o + jnp.einsum("bhqk,bhkd->bhqd", a, vj)
            return o, acc + jnp.sum(log_keep, axis=-1)

        init = (jnp.zeros((b, h, SB_BLOCK, d), jnp.float32), jnp.zeros((b, h, SB_BLOCK), jnp.float32))
        o, _ = lax.fori_loop(0, i + 1, body, init)
        return o

    out = lax.map(query_block, jnp.arange(nb))
    return jnp.moveaxis(out, 0, 2).reshape(b, h, l, d)


def sb_branch(h, w_in):
    b, l, _ = h.shape
    proj = h @ w_in
    q, k, v, z, q_mem = jnp.split(
        proj, [MAIN_WIDTH, 2 * MAIN_WIDTH, 3 * MAIN_WIDTH, 3 * MAIN_WIDTH + MIX_WIDTH], axis=-1)
    heads = lambda t: t.reshape(b, l, SB_HEADS, SB_HEAD_DIM).transpose(0, 2, 1, 3)
    o = stick_breaking_attention(heads(q), heads(k), heads(v))
    o = o.transpose(0, 2, 1, 3).reshape(b, l, MAIN_WIDTH).astype(h.dtype)
    return o, z, q_mem


def memory_attention(q_mem, mem_n, w_kv):
    b, l, _ = q_mem.shape
    m = mem_n.shape[1]
    k, v = jnp.split(mem_n @ w_kv, 2, axis=-1)
    q = q_mem.reshape(b, l, MEM_HEADS, MEM_HEAD_DIM)
    k = k.reshape(b, m, MEM_HEADS, MEM_HEAD_DIM)
    v = v.reshape(b, m, MEM_HEADS, MEM_HEAD_DIM)
    s = jnp.einsum("blhd,bmhd->bhlm", q, k).astype(jnp.float32) * (MEM_HEAD_DIM ** -0.5)
    p = jax.nn.softmax(s, axis=-1).astype(v.dtype)
    return jnp.einsum("bhlm,bmhd->blhd", p, v).reshape(b, l, MEM_WIDTH)


def mixer_output(main, z, q_mem, mem_n, w_mem_kv, out_norm, w_out):
    mem_out = memory_attention(q_mem, mem_n, w_mem_kv)
    y = jnp.concatenate([main, mem_out], axis=-1) * jax.nn.silu(z)
    return rmsnorm(y, out_norm) @ w_out


def setup_inputs(seed: int = 0) -> dict:
    key = jax.random.key(seed)
    ks = jax.random.split(key, 24)
    f = jnp.float32
    n_ssd = (DEPTH + 1) // 2
    n_sb = DEPTH // 2
    normal = lambda k, shape, s: jax.random.normal(k, shape, f) * s
    gain = lambda k, shape: 1.0 + 0.05 * jax.random.normal(k, shape, f)
    ssd_cols = SSD_CONV_DIM + SSD_HEADS + MIX_WIDTH + MEM_WIDTH
    sb_cols = 3 * MAIN_WIDTH + MIX_WIDTH + MEM_WIDTH
    dt0 = jnp.exp(jax.random.uniform(ks[8], (n_ssd, SSD_HEADS), f, math.log(1e-3), math.log(1e-1)))
    return {
        "x": normal(ks[0], (BATCH, SEQ, D_MODEL), 1.0),
        "mem": normal(ks[1], (BATCH, MEM_TOKENS, D_MODEL), 1.0),
        "mem_norm": gain(ks[2], (D_MODEL,)),
        "ssd_norm": gain(ks[3], (n_ssd, D_MODEL)),
        "ssd_w_in": normal(ks[4], (n_ssd, D_MODEL, ssd_cols), D_MODEL ** -0.5),
        "ssd_conv_w": normal(ks[5], (n_ssd, SSD_CONV, SSD_CONV_DIM), SSD_CONV ** -0.5),
        "ssd_conv_b": normal(ks[6], (n_ssd, SSD_CONV_DIM), 0.02),
        "ssd_dt_bias": dt0 + jnp.log(-jnp.expm1(-dt0)),
        "ssd_a_log": jnp.log(jax.random.uniform(ks[9], (n_ssd, SSD_HEADS), f, 1.0, 16.0)),
        "ssd_d": gain(ks[10], (n_ssd, SSD_HEADS)),
        "ssd_mem_kv": normal(ks[11], (n_ssd, D_MODEL, 2 * MEM_WIDTH), D_MODEL ** -0.5),
        "ssd_out_norm": gain(ks[12], (n_ssd, MIX_WIDTH)),
        "ssd_w_out": normal(ks[13], (n_ssd, MIX_WIDTH, D_MODEL), MIX_WIDTH ** -0.5),
        "sb_norm": gain(ks[14], (n_sb, D_MODEL)),
        "sb_w_in": normal(ks[15], (n_sb, D_MODEL, sb_cols), D_MODEL ** -0.5),
        "sb_mem_kv": normal(ks[16], (n_sb, D_MODEL, 2 * MEM_WIDTH), D_MODEL ** -0.5),
        "sb_out_norm": gain(ks[17], (n_sb, MIX_WIDTH)),
        "sb_w_out": normal(ks[18], (n_sb, MIX_WIDTH, D_MODEL), MIX_WIDTH ** -0.5),
        "final_norm": gain(ks[19], (D_MODEL,)),
    }


def reference(x, mem, mem_norm, ssd_norm, ssd_w_in, ssd_conv_w, ssd_conv_b, ssd_dt_bias, ssd_a_log,
              ssd_d, ssd_mem_kv, ssd_out_norm, ssd_w_out, sb_norm, sb_w_in, sb_mem_kv, sb_out_norm,
              sb_w_out, final_norm):
    mem_n = rmsnorm(mem, mem_norm)
    for i in range(DEPTH):
        j = i // N_MIXERS
        if i % N_MIXERS == 0:
            h = rmsnorm(x, ssd_norm[j])
            main, z, q_mem = ssd_branch(h, ssd_w_in[j], ssd_conv_w[j], ssd_conv_b[j],
                                        ssd_dt_bias[j], ssd_a_log[j], ssd_d[j])
            x = x + mixer_output(main, z, q_mem, mem_n, ssd_mem_kv[j], ssd_out_norm[j], ssd_w_out[j])
        else:
            h = rmsnorm(x, sb_norm[j])
            main, z, q_mem = sb_branch(h, sb_w_in[j])
            x = x + mixer_output(main, z, q_mem, mem_n, sb_mem_kv[j], sb_out_norm[j], sb_w_out[j])
    return rmsnorm(x, final_norm)
```

```python
import functools

import jax
import jax.numpy as jnp
from jax import lax
from jax.experimental import pallas as pl
from jax.experimental.pallas import tpu as pltpu

D_MODEL = 2048
MEM_TOKENS = 256
MIX_WIDTH = 2 * D_MODEL
MEM_HEADS = 4
MEM_WIDTH = MIX_WIDTH // 4
MEM_HEAD_DIM = MEM_WIDTH // MEM_HEADS
MAIN_WIDTH = MIX_WIDTH - MEM_WIDTH
SSD_HEAD_DIM = 64
SSD_HEADS = MAIN_WIDTH // SSD_HEAD_DIM
SSD_GROUPS = 8
SSD_HEADS_PER_GROUP = SSD_HEADS // SSD_GROUPS
SSD_STATE = 128
SSD_CONV = 4
SSD_CHUNK = 128
SSD_BC_WIDTH = SSD_GROUPS * SSD_STATE
SSD_CONV_DIM = MAIN_WIDTH + 2 * SSD_BC_WIDTH
SB_HEAD_DIM = 128
SB_HEADS = MAIN_WIDTH // SB_HEAD_DIM
EPS = 1e-6

LANES = 128
SUBLANES = 8
HEAD_PAD = LANES
VMEM_LIMIT = 56 * 1024 * 1024

BF16 = jnp.bfloat16
F32 = jnp.float32


def _cparams(semantics):
    return pltpu.CompilerParams(dimension_semantics=semantics, vmem_limit_bytes=VMEM_LIMIT)


def _rms_kernel(x_ref, g_ref, o_ref):
    x = x_ref[...]
    ms = jnp.mean(x * x, axis=-1, keepdims=True)
    o_ref[...] = (x * lax.rsqrt(ms + EPS) * g_ref[...]).astype(o_ref.dtype)


def _rmsnorm(x, g, out_dtype, tm=512):
    m, d = x.shape
    tm = min(tm, m)
    return pl.pallas_call(
        _rms_kernel,
        grid=(m // tm,),
        in_specs=[pl.BlockSpec((tm, d), lambda i: (i, 0)),
                  pl.BlockSpec((1, d), lambda i: (0, 0))],
        out_specs=pl.BlockSpec((tm, d), lambda i: (i, 0)),
        out_shape=jax.ShapeDtypeStruct((m, d), out_dtype),
        compiler_params=_cparams(("parallel",)),
        name="rmsnorm",
    )(x, g.reshape(1, d))


def _mm_kernel(x_ref, w_ref, o_ref):
    o_ref[...] = jnp.dot(x_ref[...], w_ref[...], preferred_element_type=F32).astype(o_ref.dtype)


def _mm_res_kernel(x_ref, w_ref, r_ref, o_ref):
    acc = jnp.dot(x_ref[...], w_ref[...], preferred_element_type=F32)
    o_ref[...] = (r_ref[...] + acc).astype(o_ref.dtype)


def _matmul(x, w, out_dtype, residual=None, tm=1024, tn=512):
    m, k = x.shape
    n = w.shape[1]
    tm = min(tm, m)
    tn = min(tn, n)
    in_specs = [pl.BlockSpec((tm, k), lambda i, j: (i, 0)),
                pl.BlockSpec((k, tn), lambda i, j: (0, j))]
    args = [x, w]
    body = _mm_kernel
    if residual is not None:
        in_specs.append(pl.BlockSpec((tm, tn), lambda i, j: (i, j)))
        args.append(residual)
        body = _mm_res_kernel
    return pl.pallas_call(
        body,
        grid=(m // tm, n // tn),
        in_specs=in_specs,
        out_specs=pl.BlockSpec((tm, tn), lambda i, j: (i, j)),
        out_shape=jax.ShapeDtypeStruct((m, n), out_dtype),
        compiler_params=_cparams(("parallel", "arbitrary")),
        name="matmul",
    )(*args)


def _split3(x):
    p0 = x.astype(BF16)
    r1 = x - p0.astype(F32)
    p1 = r1.astype(BF16)
    p2 = (r1 - p1.astype(F32)).astype(BF16)
    return p0, p1, p2


def _dot_f32_by_01(x, ones_rhs):
    p0, p1, p2 = _split3(x)
    d = functools.partial(jnp.dot, preferred_element_type=F32)
    return d(p0, ones_rhs) + (d(p1, ones_rhs) + d(p2, ones_rhs))


def _dot_01_by_f32(ones_lhs, x):
    p0, p1, p2 = _split3(x)
    d = functools.partial(jnp.dot, preferred_element_type=F32)
    return d(ones_lhs, p0) + (d(ones_lhs, p1) + d(ones_lhs, p2))


def _softplus(x):
    return jnp.maximum(x, 0.0) + jnp.log(1.0 + jnp.exp(-jnp.abs(x)))


def _silu(x):
    return x / (1.0 + jnp.exp(-x))


def _ssd_kernel(xbc_ref, dt_ref, cw_ref, cb_ref, dtb_ref, alog_ref, expand_ref, dskip_ref,
                y_ref, ext_ref, xc_ref, ex_ref, st_ref):
    q = SSD_CHUNK
    hp = SSD_HEAD_DIM
    gw = SSD_HEADS_PER_GROUP * hp

    @pl.when(pl.program_id(1) == 0)
    def _():
        ext_ref[0:SUBLANES, :] = jnp.zeros((SUBLANES, SSD_CONV_DIM), F32)
        st_ref[...] = jnp.zeros_like(st_ref)

    u = xbc_ref[0]
    ext_ref[SUBLANES:SUBLANES + q, :] = u
    conv = cb_ref[...] + cw_ref[SSD_CONV - 1:SSD_CONV, :] * u
    for tap in range(SSD_CONV - 1):
        shift = SSD_CONV - 1 - tap
        conv = conv + cw_ref[tap:tap + 1, :] * ext_ref[SUBLANES - shift:SUBLANES - shift + q, :]
    ext_ref[0:SUBLANES, :] = u[q - SUBLANES:q, :]
    xc_ref[...] = _silu(conv)

    dt = _softplus(dt_ref[0] + dtb_ref[...])
    d_a = dt * (-jnp.exp(alog_ref[...]))
    row = lax.broadcasted_iota(jnp.int32, (q, q), 0)
    col = lax.broadcasted_iota(jnp.int32, (q, q), 1)
    causal = row >= col
    cum = _dot_01_by_f32(causal.astype(BF16), d_a)
    cum_t = cum.T
    cum_end = cum[q - 1:q, :]
    to_end = jnp.exp(cum_end - cum) * dt
    ecum = jnp.exp(cum)

    expand = expand_ref[...]
    ex_ref[0:q, :] = _dot_f32_by_01(dt, expand)
    ex_ref[q:2 * q, :] = _dot_f32_by_01(to_end, expand)
    ex_ref[2 * q:3 * q, :] = _dot_f32_by_01(ecum, expand)

    lane = lax.broadcasted_iota(jnp.int32, (q, LANES), 1)
    low_half = lane < hp

    for g in range(SSD_GROUPS):
        bg = xc_ref[:, MAIN_WIDTH + g * SSD_STATE:MAIN_WIDTH + (g + 1) * SSD_STATE]
        cg = xc_ref[:, MAIN_WIDTH + SSD_BC_WIDTH + g * SSD_STATE:
                    MAIN_WIDTH + SSD_BC_WIDTH + (g + 1) * SSD_STATE]
        cg16 = cg.astype(BF16)
        cb = lax.dot_general(cg16, bg.astype(BF16), (((1,), (1,)), ((), ())),
                             preferred_element_type=F32)
        gs = slice(g * gw, (g + 1) * gw)
        y_off = jnp.dot(cg16, st_ref[:, gs].astype(BF16), preferred_element_type=F32)

        for pair in range(SSD_HEADS_PER_GROUP // 2):
            ws = []
            for r in (2 * pair, 2 * pair + 1):
                h = g * SSD_HEADS_PER_GROUP + r
                seg = cum[:, h:h + 1] - cum_t[h:h + 1, :]
                ws.append(cb * jnp.exp(jnp.where(causal, seg, -jnp.inf)))
            w2 = jnp.concatenate(ws, axis=1).astype(BF16)
            ps = slice(g * gw + pair * LANES, g * gw + (pair + 1) * LANES)
            xs = xc_ref[:, ps]
            xdt = xs * ex_ref[0:q, ps]
            rhs = jnp.concatenate([jnp.where(low_half, xdt, 0.0),
                                   jnp.where(low_half, 0.0, xdt)], axis=0).astype(BF16)
            y_diag = jnp.dot(w2, rhs, preferred_element_type=F32)
            yo = y_off[:, pair * LANES:(pair + 1) * LANES]
            y_ref[0, :, ps] = y_diag + ex_ref[2 * q:3 * q, ps] * yo + xs * dskip_ref[:, ps]

        xte = (xc_ref[:, gs] * ex_ref[q:2 * q, gs]).astype(BF16)
        upd = jnp.dot(bg.T.astype(BF16), xte, preferred_element_type=F32)
        st_ref[:, gs] = st_ref[:, gs] * ex_ref[3 * q - 1:3 * q, gs] + upd


def _ssd_mixer(xbc, dt_raw, conv_w, conv_b, dt_bias, a_log, d_skip, batch, seq):
    q = SSD_CHUNK
    nchunks = seq // q
    pad = HEAD_PAD - SSD_HEADS
    dtb = jnp.pad(dt_bias, (0, pad)).reshape(1, HEAD_PAD)
    alog = jnp.pad(a_log, (0, pad)).reshape(1, HEAD_PAD)
    head_of_lane = jnp.arange(MAIN_WIDTH) // SSD_HEAD_DIM
    expand = (jnp.arange(HEAD_PAD)[:, None] == head_of_lane[None, :]).astype(BF16)
    dskip = jnp.repeat(d_skip, SSD_HEAD_DIM).reshape(1, MAIN_WIDTH)
    const = lambda shape: pl.BlockSpec(shape, lambda b, c: (0,) * len(shape))
    return pl.pallas_call(
        _ssd_kernel,
        grid=(batch, nchunks),
        in_specs=[pl.BlockSpec((1, q, SSD_CONV_DIM), lambda b, c: (b, c, 0)),
                  pl.BlockSpec((1, q, HEAD_PAD), lambda b, c: (b, c, 0)),
                  const((SSD_CONV, SSD_CONV_DIM)),
                  const((1, SSD_CONV_DIM)),
                  const((1, HEAD_PAD)),
                  const((1, HEAD_PAD)),
                  const((HEAD_PAD, MAIN_WIDTH)),
                  const((1, MAIN_WIDTH))],
        out_specs=pl.BlockSpec((1, q, MAIN_WIDTH), lambda b, c: (b, c, 0)),
        out_shape=jax.ShapeDtypeStruct((batch, seq, MAIN_WIDTH), F32),
        scratch_shapes=[pltpu.VMEM((SUBLANES + q, SSD_CONV_DIM), F32),
                        pltpu.VMEM((q, SSD_CONV_DIM), F32),
                        pltpu.VMEM((3 * q, MAIN_WIDTH), F32),
                        pltpu.VMEM((SSD_STATE, MAIN_WIDTH), F32)],
        compiler_params=_cparams(("parallel", "arbitrary")),
        name="ssd_mixer",
    )(xbc.reshape(batch, seq, SSD_CONV_DIM), dt_raw.reshape(batch, seq, HEAD_PAD),
      conv_w, conv_b.reshape(1, SSD_CONV_DIM), dtb, alog, expand, dskip)


def _sb_kernel(q_ref, k_ref, v_ref, o_ref, *, tile):
    t = tile
    i = pl.program_id(2)
    scale = SB_HEAD_DIM ** -0.5
    qb = q_ref[0]
    row = lax.broadcasted_iota(jnp.int32, (t, t), 0)
    col = lax.broadcasted_iota(jnp.int32, (t, t), 1)
    strict = row > col
    later = strict.astype(BF16)

    def block(j, acc, o, diagonal):
        kj = k_ref[0, pl.ds(pl.multiple_of(j * t, t), t), :]
        vj = v_ref[0, pl.ds(pl.multiple_of(j * t, t), t), :]
        z = lax.dot_general(qb, kj, (((1,), (1,)), ((), ())), preferred_element_type=F32) * scale
        sp = _softplus(z)
        log_keep = -sp
        if diagonal:
            log_keep = jnp.where(strict, log_keep, 0.0)
        after = _dot_f32_by_01(log_keep, later)
        log_a = (z - sp) + after + acc
        a = jnp.exp(log_a)
        if diagonal:
            a = jnp.where(strict, a, 0.0)
        o = o + jnp.dot(a.astype(BF16), vj, preferred_element_type=F32)
        acc = acc + jnp.sum(log_keep, axis=1, keepdims=True)
        return acc, o

    acc0 = jnp.zeros((t, 1), F32)
    o0 = jnp.zeros((t, SB_HEAD_DIM), F32)
    acc, o = block(i, acc0, o0, True)

    def body(step, carry):
        acc, o = carry
        return block(i - 1 - step, acc, o, False)

    acc, o = lax.fori_loop(0, i, body, (acc, o))
    o_ref[0] = o


def _sb_mixer(qkv, batch, seq, tile=256):
    nq = seq // tile
    qkv3 = qkv.reshape(batch, seq, 3 * MAIN_WIDTH)
    return pl.pallas_call(
        functools.partial(_sb_kernel, tile=tile),
        grid=(batch, SB_HEADS, nq),
        in_specs=[pl.BlockSpec((1, tile, SB_HEAD_DIM), lambda b, h, i: (b, i, h)),
                  pl.BlockSpec((1, seq, SB_HEAD_DIM), lambda b, h, i: (b, 0, SB_HEADS + h)),
                  pl.BlockSpec((1, seq, SB_HEAD_DIM), lambda b, h, i: (b, 0, 2 * SB_HEADS + h))],
        out_specs=pl.BlockSpec((1, tile, SB_HEAD_DIM), lambda b, h, i: (b, i, h)),
        out_shape=jax.ShapeDtypeStruct((batch, seq, MAIN_WIDTH), F32),
        compiler_params=_cparams(("parallel", "parallel", "arbitrary")),
        name="stick_breaking",
    )(qkv3, qkv3, qkv3)


def _mix_kernel(main_ref, z_ref, qm_ref, kv_ref, g_ref, o_ref):
    scale = MEM_HEAD_DIM ** -0.5
    gate_main = _silu(z_ref[:, 0:MAIN_WIDTH])
    ym = main_ref[...] * gate_main
    ss = jnp.sum(ym * ym, axis=-1, keepdims=True)
    parts = []
    for h in range(MEM_HEADS):
        hs = slice(h * MEM_HEAD_DIM, (h + 1) * MEM_HEAD_DIM)
        kh = kv_ref[0, :, hs]
        vh = kv_ref[0, :, MEM_WIDTH + h * MEM_HEAD_DIM:MEM_WIDTH + (h + 1) * MEM_HEAD_DIM]
        s = lax.dot_general(qm_ref[:, hs], kh, (((1,), (1,)), ((), ())),
                            preferred_element_type=F32) * scale
        e = jnp.exp(s - jnp.max(s, axis=-1, keepdims=True))
        p = e * (1.0 / jnp.sum(e, axis=-1, keepdims=True))
        mo = jnp.dot(p.astype(BF16), vh, preferred_element_type=F32)
        zs = slice(MAIN_WIDTH + h * MEM_HEAD_DIM, MAIN_WIDTH + (h + 1) * MEM_HEAD_DIM)
        part = mo * _silu(z_ref[:, zs])
        ss = ss + jnp.sum(part * part, axis=-1, keepdims=True)
        parts.append(part)
    inv = lax.rsqrt(ss * (1.0 / MIX_WIDTH) + EPS)
    o_ref[:, 0:MAIN_WIDTH] = (ym * inv * g_ref[:, 0:MAIN_WIDTH]).astype(o_ref.dtype)
    for h in range(MEM_HEADS):
        zs = slice(MAIN_WIDTH + h * MEM_HEAD_DIM, MAIN_WIDTH + (h + 1) * MEM_HEAD_DIM)
        o_ref[:, zs] = (parts[h] * inv * g_ref[:, zs]).astype(o_ref.dtype)


def _mix(main, z, q_mem, kv, out_norm, seq, tm=256):
    m = main.shape[0]
    tiles_per_batch = seq // tm
    return pl.pallas_call(
        _mix_kernel,
        grid=(m // tm,),
        in_specs=[pl.BlockSpec((tm, MAIN_WIDTH), lambda i: (i, 0)),
                  pl.BlockSpec((tm, MIX_WIDTH), lambda i: (i, 0)),
                  pl.BlockSpec((tm, MEM_WIDTH), lambda i: (i, 0)),
                  pl.BlockSpec((1, MEM_TOKENS, 2 * MEM_WIDTH), lambda i: (i // tiles_per_batch, 0, 0)),
                  pl.BlockSpec((1, MIX_WIDTH), lambda i: (0, 0))],
        out_specs=pl.BlockSpec((tm, MIX_WIDTH), lambda i: (i, 0)),
        out_shape=jax.ShapeDtypeStruct((m, MIX_WIDTH), BF16),
        compiler_params=_cparams(("parallel",)),
        name="memory_gate_norm",
    )(main, z, q_mem, kv, out_norm.reshape(1, MIX_WIDTH))


def kernel(x, mem, mem_norm, ssd_norm, ssd_w_in, ssd_conv_w, ssd_conv_b, ssd_dt_bias, ssd_a_log, ssd_d, ssd_mem_kv, ssd_out_norm, ssd_w_out, sb_norm, sb_w_in, sb_mem_kv, sb_out_norm, sb_w_out, final_norm):
    batch, seq, d = x.shape
    m = batch * seq
    depth = ssd_w_in.shape[0] + sb_w_in.shape[0]
    xf = x.reshape(m, d)
    mem_n = _rmsnorm(mem.reshape(batch * MEM_TOKENS, d), mem_norm, BF16)

    def mixer_tail(xf, main, z, q_mem, w_kv, out_norm, w_out):
        kv = _matmul(mem_n, w_kv.astype(BF16), BF16).reshape(batch, MEM_TOKENS, 2 * MEM_WIDTH)
        yn = _mix(main, z, q_mem, kv, out_norm, seq)
        return _matmul(yn, w_out.astype(BF16), F32, residual=xf, tm=512)

    for layer in range(depth):
        j = layer // 2
        if layer % 2 == 0:
            h = _rmsnorm(xf, ssd_norm[j], BF16)
            w = ssd_w_in[j].astype(BF16)
            c0, c1, c2 = SSD_CONV_DIM, SSD_CONV_DIM + SSD_HEADS, SSD_CONV_DIM + SSD_HEADS + MIX_WIDTH
            xbc = _matmul(h, w[:, :c0], F32)
            w_dt = jnp.pad(w[:, c0:c1], ((0, 0), (0, HEAD_PAD - SSD_HEADS)))
            dt_raw = _matmul(h, w_dt, F32)
            z = _matmul(h, w[:, c1:c2], F32)
            q_mem = _matmul(h, w[:, c2:], BF16)
            main = _ssd_mixer(xbc, dt_raw, ssd_conv_w[j], ssd_conv_b[j], ssd_dt_bias[j],
                              ssd_a_log[j], ssd_d[j], batch, seq).reshape(m, MAIN_WIDTH)
            xf = mixer_tail(xf, main, z, q_mem, ssd_mem_kv[j], ssd_out_norm[j], ssd_w_out[j])
        else:
            h = _rmsnorm(xf, sb_norm[j], BF16)
            w = sb_w_in[j].astype(BF16)
            c0, c1 = 3 * MAIN_WIDTH, 3 * MAIN_WIDTH + MIX_WIDTH
            qkv = _matmul(h, w[:, :c0], BF16)
            z = _matmul(h, w[:, c0:c1], F32)
            q_mem = _matmul(h, w[:, c1:], BF16)
            main = _sb_mixer(qkv, batch, seq).reshape(m, MAIN_WIDTH)
            xf = mixer_tail(xf, main, z, q_mem, sb_mem_kv[j], sb_out_norm[j], sb_w_out[j])

    return _rmsnorm(xf, final_norm, F32).reshape(batch, seq, d)
```

```python
import functools

import jax
import jax.numpy as jnp
from jax import lax
from jax.experimental import pallas as pl
from jax.experimental.pallas import tpu as pltpu

D_MODEL = 2048
MEM_TOKENS = 256
MIX_WIDTH = 2 * D_MODEL
MEM_HEADS = 4
MEM_WIDTH = MIX_WIDTH // 4
MEM_HEAD_DIM = MEM_WIDTH // MEM_HEADS
MAIN_WIDTH = MIX_WIDTH - MEM_WIDTH
SSD_HEAD_DIM = 64
SSD_HEADS = MAIN_WIDTH // SSD_HEAD_DIM
SSD_GROUPS = 8
SSD_HEADS_PER_GROUP = SSD_HEADS // SSD_GROUPS
SSD_STATE = 128
SSD_CONV = 4
SSD_CHUNK = 128
SSD_BC_WIDTH = SSD_GROUPS * SSD_STATE
SSD_CONV_DIM = MAIN_WIDTH + 2 * SSD_BC_WIDTH
SB_HEAD_DIM = 128
SB_HEADS = MAIN_WIDTH // SB_HEAD_DIM
EPS = 1e-6
LOG2E = 1.4426950408889634
SB_SUBBLOCKS = 2
SB_EXHAUSTED_LOG2 = -160.0

LANES = 128
SUBLANES = 8
HEAD_PAD = LANES
VMEM_LIMIT = 56 * 1024 * 1024

BF16 = jnp.bfloat16
F32 = jnp.float32


def _cparams(semantics):
    return pltpu.CompilerParams(dimension_semantics=semantics, vmem_limit_bytes=VMEM_LIMIT)


def _rms_kernel(x_ref, g_ref, o_ref):
    x = x_ref[...]
    ms = jnp.mean(x * x, axis=-1, keepdims=True)
    o_ref[...] = (x * lax.rsqrt(ms + EPS) * g_ref[...]).astype(o_ref.dtype)


def _rmsnorm(x, g, out_dtype, tm=512):
    m, d = x.shape
    tm = min(tm, m)
    return pl.pallas_call(
        _rms_kernel,
        grid=(m // tm,),
        in_specs=[pl.BlockSpec((tm, d), lambda i: (i, 0)),
                  pl.BlockSpec((1, d), lambda i: (0, 0))],
        out_specs=pl.BlockSpec((tm, d), lambda i: (i, 0)),
        out_shape=jax.ShapeDtypeStruct((m, d), out_dtype),
        compiler_params=_cparams(("parallel",)),
        name="rmsnorm",
    )(x, g.reshape(1, d))


def _mm_kernel(x_ref, w_ref, o_ref):
    o_ref[...] = jnp.dot(x_ref[...], w_ref[...], preferred_element_type=F32).astype(o_ref.dtype)


def _mm_res_kernel(x_ref, w_ref, r_ref, o_ref):
    acc = jnp.dot(x_ref[...], w_ref[...], preferred_element_type=F32)
    o_ref[...] = (r_ref[...] + acc).astype(o_ref.dtype)


def _matmul(x, w, out_dtype, residual=None, tm=1024, tn=512):
    m, k = x.shape
    n = w.shape[1]
    tm = min(tm, m)
    tn = min(tn, n)
    in_specs = [pl.BlockSpec((tm, k), lambda i, j: (i, 0)),
                pl.BlockSpec((k, tn), lambda i, j: (0, j))]
    args = [x, w]
    body = _mm_kernel
    if residual is not None:
        in_specs.append(pl.BlockSpec((tm, tn), lambda i, j: (i, j)))
        args.append(residual)
        body = _mm_res_kernel
    return pl.pallas_call(
        body,
        grid=(m // tm, n // tn),
        in_specs=in_specs,
        out_specs=pl.BlockSpec((tm, tn), lambda i, j: (i, j)),
        out_shape=jax.ShapeDtypeStruct((m, n), out_dtype),
        compiler_params=_cparams(("parallel", "arbitrary")),
        name="matmul",
    )(*args)


def _split3(x):
    p0 = x.astype(BF16)
    r1 = x - p0.astype(F32)
    p1 = r1.astype(BF16)
    p2 = (r1 - p1.astype(F32)).astype(BF16)
    return p0, p1, p2


def _dot_f32_by_01(x, ones_rhs):
    p0, p1, p2 = _split3(x)
    d = functools.partial(jnp.dot, preferred_element_type=F32)
    return d(p0, ones_rhs) + (d(p1, ones_rhs) + d(p2, ones_rhs))


def _dot_01_by_f32(ones_lhs, x):
    p0, p1, p2 = _split3(x)
    d = functools.partial(jnp.dot, preferred_element_type=F32)
    return d(ones_lhs, p0) + (d(ones_lhs, p1) + d(ones_lhs, p2))


def _softplus(x):
    return jnp.maximum(x, 0.0) + jnp.log(1.0 + jnp.exp(-jnp.abs(x)))


def _silu(x):
    return x / (1.0 + jnp.exp(-x))


def _ssd_kernel(xbc_ref, dt_ref, cw_ref, cb_ref, dtb_ref, alog_ref, expand_ref, dskip_ref,
                y_ref, ext_ref, xc_ref, ex_ref, st_ref):
    q = SSD_CHUNK
    hp = SSD_HEAD_DIM
    gw = SSD_HEADS_PER_GROUP * hp

    @pl.when(pl.program_id(1) == 0)
    def _():
        ext_ref[0:SUBLANES, :] = jnp.zeros((SUBLANES, SSD_CONV_DIM), F32)
        st_ref[...] = jnp.zeros_like(st_ref)

    u = xbc_ref[0]
    ext_ref[SUBLANES:SUBLANES + q, :] = u
    conv = cb_ref[...] + cw_ref[SSD_CONV - 1:SSD_CONV, :] * u
    for tap in range(SSD_CONV - 1):
        shift = SSD_CONV - 1 - tap
        conv = conv + cw_ref[tap:tap + 1, :] * ext_ref[SUBLANES - shift:SUBLANES - shift + q, :]
    ext_ref[0:SUBLANES, :] = u[q - SUBLANES:q, :]
    xc_ref[...] = _silu(conv)

    dt = _softplus(dt_ref[0] + dtb_ref[...])
    d_a = dt * (-jnp.exp(alog_ref[...]))
    row = lax.broadcasted_iota(jnp.int32, (q, q), 0)
    col = lax.broadcasted_iota(jnp.int32, (q, q), 1)
    causal = row >= col
    cum = _dot_01_by_f32(causal.astype(BF16), d_a)
    cum_t = cum.T
    cum_end = cum[q - 1:q, :]
    to_end = jnp.exp(cum_end - cum) * dt
    ecum = jnp.exp(cum)

    expand = expand_ref[...]
    ex_ref[0:q, :] = _dot_f32_by_01(dt, expand)
    ex_ref[q:2 * q, :] = _dot_f32_by_01(to_end, expand)
    ex_ref[2 * q:3 * q, :] = _dot_f32_by_01(ecum, expand)

    lane = lax.broadcasted_iota(jnp.int32, (q, LANES), 1)
    low_half = lane < hp

    for g in range(SSD_GROUPS):
        bg = xc_ref[:, MAIN_WIDTH + g * SSD_STATE:MAIN_WIDTH + (g + 1) * SSD_STATE]
        cg = xc_ref[:, MAIN_WIDTH + SSD_BC_WIDTH + g * SSD_STATE:
                    MAIN_WIDTH + SSD_BC_WIDTH + (g + 1) * SSD_STATE]
        cg16 = cg.astype(BF16)
        cb = lax.dot_general(cg16, bg.astype(BF16), (((1,), (1,)), ((), ())),
                             preferred_element_type=F32)
        gs = slice(g * gw, (g + 1) * gw)
        y_off = jnp.dot(cg16, st_ref[:, gs].astype(BF16), preferred_element_type=F32)

        for pair in range(SSD_HEADS_PER_GROUP // 2):
            ws = []
            for r in (2 * pair, 2 * pair + 1):
                h = g * SSD_HEADS_PER_GROUP + r
                seg = cum[:, h:h + 1] - cum_t[h:h + 1, :]
                ws.append(cb * jnp.exp(jnp.where(causal, seg, -jnp.inf)))
            w2 = jnp.concatenate(ws, axis=1).astype(BF16)
            ps = slice(g * gw + pair * LANES, g * gw + (pair + 1) * LANES)
            xs = xc_ref[:, ps]
            xdt = xs * ex_ref[0:q, ps]
            rhs = jnp.concatenate([jnp.where(low_half, xdt, 0.0),
                                   jnp.where(low_half, 0.0, xdt)], axis=0).astype(BF16)
            y_diag = jnp.dot(w2, rhs, preferred_element_type=F32)
            yo = y_off[:, pair * LANES:(pair + 1) * LANES]
            y_ref[0, :, ps] = y_diag + ex_ref[2 * q:3 * q, ps] * yo + xs * dskip_ref[:, ps]

        xte = (xc_ref[:, gs] * ex_ref[q:2 * q, gs]).astype(BF16)
        upd = jnp.dot(bg.T.astype(BF16), xte, preferred_element_type=F32)
        st_ref[:, gs] = st_ref[:, gs] * ex_ref[3 * q - 1:3 * q, gs] + upd


def _ssd_mixer(xbc, dt_raw, conv_w, conv_b, dt_bias, a_log, d_skip, batch, seq):
    q = SSD_CHUNK
    nchunks = seq // q
    pad = HEAD_PAD - SSD_HEADS
    dtb = jnp.pad(dt_bias, (0, pad)).reshape(1, HEAD_PAD)
    alog = jnp.pad(a_log, (0, pad)).reshape(1, HEAD_PAD)
    head_of_lane = jnp.arange(MAIN_WIDTH) // SSD_HEAD_DIM
    expand = (jnp.arange(HEAD_PAD)[:, None] == head_of_lane[None, :]).astype(BF16)
    dskip = jnp.repeat(d_skip, SSD_HEAD_DIM).reshape(1, MAIN_WIDTH)
    const = lambda shape: pl.BlockSpec(shape, lambda b, c: (0,) * len(shape))
    return pl.pallas_call(
        _ssd_kernel,
        grid=(batch, nchunks),
        in_specs=[pl.BlockSpec((1, q, SSD_CONV_DIM), lambda b, c: (b, c, 0)),
                  pl.BlockSpec((1, q, HEAD_PAD), lambda b, c: (b, c, 0)),
                  const((SSD_CONV, SSD_CONV_DIM)),
                  const((1, SSD_CONV_DIM)),
                  const((1, HEAD_PAD)),
                  const((1, HEAD_PAD)),
                  const((HEAD_PAD, MAIN_WIDTH)),
                  const((1, MAIN_WIDTH))],
        out_specs=pl.BlockSpec((1, q, MAIN_WIDTH), lambda b, c: (b, c, 0)),
        out_shape=jax.ShapeDtypeStruct((batch, seq, MAIN_WIDTH), F32),
        scratch_shapes=[pltpu.VMEM((SUBLANES + q, SSD_CONV_DIM), F32),
                        pltpu.VMEM((q, SSD_CONV_DIM), F32),
                        pltpu.VMEM((3 * q, MAIN_WIDTH), F32),
                        pltpu.VMEM((SSD_STATE, MAIN_WIDTH), F32)],
        compiler_params=_cparams(("parallel", "arbitrary")),
        name="ssd_mixer",
    )(xbc.reshape(batch, seq, SSD_CONV_DIM), dt_raw.reshape(batch, seq, HEAD_PAD),
      conv_w, conv_b.reshape(1, SSD_CONV_DIM), dtb, alog, expand, dskip)


def _sb_kernel(q_ref, k_ref, v_ref, o_ref, *, tile):
    t = tile
    zscale = SB_HEAD_DIM ** -0.5 * LOG2E
    row = lax.broadcasted_iota(jnp.int32, (t, t), 0)
    col = lax.broadcasted_iota(jnp.int32, (t, t), 1)
    strict = row > col
    later = strict.astype(BF16)
    later2 = jnp.concatenate([later, later], axis=0)

    def scores(qb, j):
        kj = k_ref[0, pl.ds(pl.multiple_of(j * t, t), t), :]
        z2 = lax.dot_general(qb, kj, (((1,), (1,)), ((), ())), preferred_element_type=F32) * zscale
        neg_abs = lax.bitcast_convert_type(
            lax.bitcast_convert_type(z2, jnp.uint32) | jnp.uint32(0x80000000), F32)
        log_sig = jnp.minimum(z2, 0.0) - jnp.log2(1.0 + jnp.exp2(neg_abs))
        return log_sig, log_sig - z2

    def finish(j, log_sig, log_keep, acc, o, diagonal):
        vj = v_ref[0, pl.ds(pl.multiple_of(j * t, t), t), :]
        if diagonal:
            log_keep = jnp.where(strict, log_keep, 0.0)
        hi = log_keep.astype(BF16)
        lo = (log_keep - hi.astype(F32)).astype(BF16)
        after = jnp.dot(jnp.concatenate([hi, lo], axis=1), later2, preferred_element_type=F32)
        a = jnp.exp2(log_sig + after + acc)
        if diagonal:
            a = jnp.where(strict, a, 0.0)
        o = o + jnp.dot(a.astype(BF16), vj, preferred_element_type=F32)
        acc = acc + jnp.sum(log_keep, axis=1, keepdims=True)
        return acc, o

    def pair(qb, j_a, j_b, use_b, acc, o, diagonal):
        j_b = jnp.maximum(j_b, 0)
        sig_a, keep_a = scores(qb, j_a)
        sig_b, keep_b = scores(qb, j_b)
        acc, o = finish(j_a, sig_a, keep_a, acc, o, diagonal)
        acc_b, o_b = finish(j_b, sig_b, keep_b, acc, o, False)
        return jnp.where(use_b, acc_b, acc), jnp.where(use_b, o_b, o)

    started = []
    for r in range(SB_SUBBLOCKS):
        i = pl.program_id(2) * SB_SUBBLOCKS + r
        qb = q_ref[0, r * t:(r + 1) * t, :]
        acc, o = pair(qb, i, i - 1, i >= 1, jnp.zeros((t, 1), F32), jnp.zeros((t, SB_HEAD_DIM), F32), True)
        started.append((i, qb, acc, o))

    for r, (i, qb, acc, o) in enumerate(started):
        def unfinished(carry, i=i):
            step, acc, _ = carry
            return jnp.logical_and(step < i // 2, jnp.max(acc) > SB_EXHAUSTED_LOG2)

        def body(carry, i=i, qb=qb):
            step, acc, o = carry
            j_a = i - 2 - 2 * step
            acc, o = pair(qb, j_a, j_a - 1, j_a >= 1, acc, o, False)
            return step + 1, acc, o

        _, _, o = lax.while_loop(unfinished, body, (jnp.int32(0), acc, o))
        o_ref[0, r * t:(r + 1) * t, :] = o


def _sb_mixer(qkv, batch, seq, tile=256):
    rows = SB_SUBBLOCKS * tile
    nq = seq // rows
    qkv3 = qkv.reshape(batch, seq, 3 * MAIN_WIDTH)
    return pl.pallas_call(
        functools.partial(_sb_kernel, tile=tile),
        grid=(batch, SB_HEADS, nq),
        in_specs=[pl.BlockSpec((1, rows, SB_HEAD_DIM), lambda b, h, i: (b, i, h)),
                  pl.BlockSpec((1, seq, SB_HEAD_DIM), lambda b, h, i: (b, 0, SB_HEADS + h)),
                  pl.BlockSpec((1, seq, SB_HEAD_DIM), lambda b, h, i: (b, 0, 2 * SB_HEADS + h))],
        out_specs=pl.BlockSpec((1, rows, SB_HEAD_DIM), lambda b, h, i: (b, i, h)),
        out_shape=jax.ShapeDtypeStruct((batch, seq, MAIN_WIDTH), F32),
        compiler_params=_cparams(("parallel", "parallel", "arbitrary")),
        name="stick_breaking",
    )(qkv3, qkv3, qkv3)


def _mix_kernel(main_ref, z_ref, qm_ref, kv_ref, g_ref, o_ref):
    scale = MEM_HEAD_DIM ** -0.5
    gate_main = _silu(z_ref[:, 0:MAIN_WIDTH])
    ym = main_ref[...] * gate_main
    ss = jnp.sum(ym * ym, axis=-1, keepdims=True)
    parts = []
    for h in range(MEM_HEADS):
        hs = slice(h * MEM_HEAD_DIM, (h + 1) * MEM_HEAD_DIM)
        kh = kv_ref[0, :, hs]
        vh = kv_ref[0, :, MEM_WIDTH + h * MEM_HEAD_DIM:MEM_WIDTH + (h + 1) * MEM_HEAD_DIM]
        s = lax.dot_general(qm_ref[:, hs], kh, (((1,), (1,)), ((), ())),
                            preferred_element_type=F32) * scale
        e = jnp.exp(s - jnp.max(s, axis=-1, keepdims=True))
        p = e * (1.0 / jnp.sum(e, axis=-1, keepdims=True))
        mo = jnp.dot(p.astype(BF16), vh, preferred_element_type=F32)
        zs = slice(MAIN_WIDTH + h * MEM_HEAD_DIM, MAIN_WIDTH + (h + 1) * MEM_HEAD_DIM)
        part = mo * _silu(z_ref[:, zs])
        ss = ss + jnp.sum(part * part, axis=-1, keepdims=True)
        parts.append(part)
    inv = lax.rsqrt(ss * (1.0 / MIX_WIDTH) + EPS)
    o_ref[:, 0:MAIN_WIDTH] = (ym * inv * g_ref[:, 0:MAIN_WIDTH]).astype(o_ref.dtype)
    for h in range(MEM_HEADS):
        zs = slice(MAIN_WIDTH + h * MEM_HEAD_DIM, MAIN_WIDTH + (h + 1) * MEM_HEAD_DIM)
        o_ref[:, zs] = (parts[h] * inv * g_ref[:, zs]).astype(o_ref.dtype)


def _mix(main, z, q_mem, kv, out_norm, seq, tm=256):
    m = main.shape[0]
    tiles_per_batch = seq // tm
    return pl.pallas_call(
        _mix_kernel,
        grid=(m // tm,),
        in_specs=[pl.BlockSpec((tm, MAIN_WIDTH), lambda i: (i, 0)),
                  pl.BlockSpec((tm, MIX_WIDTH), lambda i: (i, 0)),
                  pl.BlockSpec((tm, MEM_WIDTH), lambda i: (i, 0)),
                  pl.BlockSpec((1, MEM_TOKENS, 2 * MEM_WIDTH), lambda i: (i // tiles_per_batch, 0, 0)),
                  pl.BlockSpec((1, MIX_WIDTH), lambda i: (0, 0))],
        out_specs=pl.BlockSpec((tm, MIX_WIDTH), lambda i: (i, 0)),
        out_shape=jax.ShapeDtypeStruct((m, MIX_WIDTH), BF16),
        compiler_params=_cparams(("parallel",)),
        name="memory_gate_norm",
    )(main, z, q_mem, kv, out_norm.reshape(1, MIX_WIDTH))


def kernel(x, mem, mem_norm, ssd_norm, ssd_w_in, ssd_conv_w, ssd_conv_b, ssd_dt_bias, ssd_a_log, ssd_d, ssd_mem_kv, ssd_out_norm, ssd_w_out, sb_norm, sb_w_in, sb_mem_kv, sb_out_norm, sb_w_out, final_norm):
    batch, seq, d = x.shape
    m = batch * seq
    depth = ssd_w_in.shape[0] + sb_w_in.shape[0]
    xf = x.reshape(m, d)
    mem_n = _rmsnorm(mem.reshape(batch * MEM_TOKENS, d), mem_norm, BF16)

    def mixer_tail(xf, main, z, q_mem, w_kv, out_norm, w_out):
        kv = _matmul(mem_n, w_kv.astype(BF16), BF16).reshape(batch, MEM_TOKENS, 2 * MEM_WIDTH)
        yn = _mix(main, z, q_mem, kv, out_norm, seq)
        return _matmul(yn, w_out.astype(BF16), F32, residual=xf, tm=512)

    for layer in range(depth):
        j = layer // 2
        if layer % 2 == 0:
            h = _rmsnorm(xf, ssd_norm[j], BF16)
            w = ssd_w_in[j].astype(BF16)
            c0, c1, c2 = SSD_CONV_DIM, SSD_CONV_DIM + SSD_HEADS, SSD_CONV_DIM + SSD_HEADS + MIX_WIDTH
            xbc = _matmul(h, w[:, :c0], F32)
            w_dt = jnp.pad(w[:, c0:c1], ((0, 0), (0, HEAD_PAD - SSD_HEADS)))
            dt_raw = _matmul(h, w_dt, F32)
            z = _matmul(h, w[:, c1:c2], F32)
            q_mem = _matmul(h, w[:, c2:], BF16)
            main = _ssd_mixer(xbc, dt_raw, ssd_conv_w[j], ssd_conv_b[j], ssd_dt_bias[j],
                              ssd_a_log[j], ssd_d[j], batch, seq).reshape(m, MAIN_WIDTH)
            xf = mixer_tail(xf, main, z, q_mem, ssd_mem_kv[j], ssd_out_norm[j], ssd_w_out[j])
        else:
            h = _rmsnorm(xf, sb_norm[j], BF16)
            w = sb_w_in[j].astype(BF16)
            c0, c1 = 3 * MAIN_WIDTH, 3 * MAIN_WIDTH + MIX_WIDTH
            qkv = _matmul(h, w[:, :c0], BF16)
            z = _matmul(h, w[:, c0:c1], F32)
            q_mem = _matmul(h, w[:, c1:], BF16)
            main = _sb_mixer(qkv, batch, seq).reshape(m, MAIN_WIDTH)
            xf = mixer_tail(xf, main, z, q_mem, sb_mem_kv[j], sb_out_norm[j], sb_w_out[j])

    return _rmsnorm(xf, final_norm, F32).reshape(batch, seq, d)
```

```python
import functools

import jax
import jax.numpy as jnp
from jax import lax
from jax.experimental import pallas as pl
from jax.experimental.pallas import tpu as pltpu

D_MODEL = 2048
MEM_TOKENS = 256
MIX_WIDTH = 2 * D_MODEL
MEM_HEADS = 4
MEM_WIDTH = MIX_WIDTH // 4
MEM_HEAD_DIM = MEM_WIDTH // MEM_HEADS
MAIN_WIDTH = MIX_WIDTH - MEM_WIDTH
SSD_HEAD_DIM = 64
SSD_HEADS = MAIN_WIDTH // SSD_HEAD_DIM
SSD_GROUPS = 8
SSD_HEADS_PER_GROUP = SSD_HEADS // SSD_GROUPS
SSD_STATE = 128
SSD_CONV = 4
SSD_CHUNK = 128
SSD_BC_WIDTH = SSD_GROUPS * SSD_STATE
SSD_CONV_DIM = MAIN_WIDTH + 2 * SSD_BC_WIDTH
SB_HEAD_DIM = 128
SB_HEADS = MAIN_WIDTH // SB_HEAD_DIM
EPS = 1e-6
LOG2E = 1.4426950408889634
SB_SUBBLOCKS = 4
SB_EXHAUSTED_LOG2 = -160.0

LANES = 128
SUBLANES = 8
HEAD_PAD = LANES
VMEM_LIMIT = 56 * 1024 * 1024

BF16 = jnp.bfloat16
F32 = jnp.float32


def _cparams(semantics):
    return pltpu.CompilerParams(dimension_semantics=semantics, vmem_limit_bytes=VMEM_LIMIT)


def _rms_kernel(x_ref, g_ref, o_ref):
    x = x_ref[...]
    ms = jnp.mean(x * x, axis=-1, keepdims=True)
    o_ref[...] = (x * lax.rsqrt(ms + EPS) * g_ref[...]).astype(o_ref.dtype)


def _rmsnorm(x, g, out_dtype, tm=512):
    m, d = x.shape
    tm = min(tm, m)
    return pl.pallas_call(
        _rms_kernel,
        grid=(m // tm,),
        in_specs=[pl.BlockSpec((tm, d), lambda i: (i, 0)),
                  pl.BlockSpec((1, d), lambda i: (0, 0))],
        out_specs=pl.BlockSpec((tm, d), lambda i: (i, 0)),
        out_shape=jax.ShapeDtypeStruct((m, d), out_dtype),
        compiler_params=_cparams(("parallel",)),
        name="rmsnorm",
    )(x, g.reshape(1, d))


def _mm_kernel(x_ref, w_ref, o_ref):
    o_ref[...] = jnp.dot(x_ref[...], w_ref[...], preferred_element_type=F32).astype(o_ref.dtype)


def _mm_res_kernel(x_ref, w_ref, r_ref, o_ref):
    acc = jnp.dot(x_ref[...], w_ref[...], preferred_element_type=F32)
    o_ref[...] = (r_ref[...] + acc).astype(o_ref.dtype)


def _matmul(x, w, out_dtype, residual=None, tm=1024, tn=512):
    m, k = x.shape
    n = w.shape[1]
    tm = min(tm, m)
    tn = min(tn, n)
    in_specs = [pl.BlockSpec((tm, k), lambda i, j: (i, 0)),
                pl.BlockSpec((k, tn), lambda i, j: (0, j))]
    args = [x, w]
    body = _mm_kernel
    if residual is not None:
        in_specs.append(pl.BlockSpec((tm, tn), lambda i, j: (i, j)))
        args.append(residual)
        body = _mm_res_kernel
    return pl.pallas_call(
        body,
        grid=(m // tm, n // tn),
        in_specs=in_specs,
        out_specs=pl.BlockSpec((tm, tn), lambda i, j: (i, j)),
        out_shape=jax.ShapeDtypeStruct((m, n), out_dtype),
        compiler_params=_cparams(("parallel", "arbitrary")),
        name="matmul",
    )(*args)


def _proj_kernel(x_ref, w_ref, o_ref, wb_ref):
    @pl.when(pl.program_id(1) == 0)
    def _():
        wb_ref[...] = w_ref[...].astype(BF16)

    o_ref[...] = jnp.dot(x_ref[...], wb_ref[...], preferred_element_type=F32).astype(o_ref.dtype)


def _project(x, w, layer, col0, n, out_dtype, tm=1024, tn=1024):
    m, k = x.shape
    tm = min(tm, m)
    tn = min(tn, n)
    assert col0 % tn == 0 and n % tn == 0 and m % tm == 0
    j0 = col0 // tn
    return pl.pallas_call(
        _proj_kernel,
        grid=(n // tn, m // tm),
        in_specs=[pl.BlockSpec((tm, k), lambda j, i: (i, 0)),
                  pl.BlockSpec((None, k, tn), lambda j, i: (layer, 0, j0 + j))],
        out_specs=pl.BlockSpec((tm, tn), lambda j, i: (i, j)),
        out_shape=jax.ShapeDtypeStruct((m, n), out_dtype),
        scratch_shapes=[pltpu.VMEM((k, tn), BF16)],
        compiler_params=_cparams(("parallel", "arbitrary")),
        name="projection",
    )(x, w)


def _split3(x):
    p0 = x.astype(BF16)
    r1 = x - p0.astype(F32)
    p1 = r1.astype(BF16)
    p2 = (r1 - p1.astype(F32)).astype(BF16)
    return p0, p1, p2


def _dot_f32_by_01(x, ones_rhs):
    p0, p1, p2 = _split3(x)
    d = functools.partial(jnp.dot, preferred_element_type=F32)
    return d(p0, ones_rhs) + (d(p1, ones_rhs) + d(p2, ones_rhs))


def _dot_01_by_f32(ones_lhs, x):
    p0, p1, p2 = _split3(x)
    d = functools.partial(jnp.dot, preferred_element_type=F32)
    return d(ones_lhs, p0) + (d(ones_lhs, p1) + d(ones_lhs, p2))


def _softplus(x):
    return jnp.maximum(x, 0.0) + jnp.log(1.0 + jnp.exp(-jnp.abs(x)))


def _silu(x):
    return x / (1.0 + jnp.exp(-x))


def _ssd_kernel(xbc_ref, dt_ref, cw_ref, cb_ref, dtb_ref, alog_ref, expand_ref, dskip_ref,
                y_ref, ext_ref, xc_ref, ex_ref, st_ref):
    q = SSD_CHUNK
    hp = SSD_HEAD_DIM
    gw = SSD_HEADS_PER_GROUP * hp

    @pl.when(pl.program_id(1) == 0)
    def _():
        ext_ref[0:SUBLANES, :] = jnp.zeros((SUBLANES, SSD_CONV_DIM), F32)
        st_ref[...] = jnp.zeros_like(st_ref)

    u = xbc_ref[0]
    ext_ref[SUBLANES:SUBLANES + q, :] = u
    conv = cb_ref[...] + cw_ref[SSD_CONV - 1:SSD_CONV, :] * u
    for tap in range(SSD_CONV - 1):
        shift = SSD_CONV - 1 - tap
        conv = conv + cw_ref[tap:tap + 1, :] * ext_ref[SUBLANES - shift:SUBLANES - shift + q, :]
    ext_ref[0:SUBLANES, :] = u[q - SUBLANES:q, :]
    xc_ref[...] = _silu(conv)

    dt = _softplus(dt_ref[0] + dtb_ref[...])
    d_a = dt * (-jnp.exp(alog_ref[...]))
    row = lax.broadcasted_iota(jnp.int32, (q, q), 0)
    col = lax.broadcasted_iota(jnp.int32, (q, q), 1)
    causal = row >= col
    cum = _dot_01_by_f32(causal.astype(BF16), d_a)
    cum_t = cum.T
    cum_end = cum[q - 1:q, :]
    to_end = jnp.exp(cum_end - cum) * dt
    ecum = jnp.exp(cum)

    expand = expand_ref[...]
    ex_ref[0:q, :] = _dot_f32_by_01(dt, expand)
    ex_ref[q:2 * q, :] = _dot_f32_by_01(to_end, expand)
    ex_ref[2 * q:3 * q, :] = _dot_f32_by_01(ecum, expand)

    lane = lax.broadcasted_iota(jnp.int32, (q, LANES), 1)
    low_half = lane < hp

    for g in range(SSD_GROUPS):
        bg = xc_ref[:, MAIN_WIDTH + g * SSD_STATE:MAIN_WIDTH + (g + 1) * SSD_STATE]
        cg = xc_ref[:, MAIN_WIDTH + SSD_BC_WIDTH + g * SSD_STATE:
                    MAIN_WIDTH + SSD_BC_WIDTH + (g + 1) * SSD_STATE]
        cg16 = cg.astype(BF16)
        cb = lax.dot_general(cg16, bg.astype(BF16), (((1,), (1,)), ((), ())),
                             preferred_element_type=F32)
        gs = slice(g * gw, (g + 1) * gw)
        y_off = jnp.dot(cg16, st_ref[:, gs].astype(BF16), preferred_element_type=F32)

        for pair in range(SSD_HEADS_PER_GROUP // 2):
            ws = []
            for r in (2 * pair, 2 * pair + 1):
                h = g * SSD_HEADS_PER_GROUP + r
                seg = cum[:, h:h + 1] - cum_t[h:h + 1, :]
                ws.append(cb * jnp.exp(jnp.where(causal, seg, -jnp.inf)))
            w2 = jnp.concatenate(ws, axis=1).astype(BF16)
            ps = slice(g * gw + pair * LANES, g * gw + (pair + 1) * LANES)
            xs = xc_ref[:, ps]
            xdt = xs * ex_ref[0:q, ps]
            rhs = jnp.concatenate([jnp.where(low_half, xdt, 0.0),
                                   jnp.where(low_half, 0.0, xdt)], axis=0).astype(BF16)
            y_diag = jnp.dot(w2, rhs, preferred_element_type=F32)
            yo = y_off[:, pair * LANES:(pair + 1) * LANES]
            y_ref[0, :, ps] = y_diag + ex_ref[2 * q:3 * q, ps] * yo + xs * dskip_ref[:, ps]

        xte = (xc_ref[:, gs] * ex_ref[q:2 * q, gs]).astype(BF16)
        upd = jnp.dot(bg.T.astype(BF16), xte, preferred_element_type=F32)
        st_ref[:, gs] = st_ref[:, gs] * ex_ref[3 * q - 1:3 * q, gs] + upd


def _ssd_mixer(xbc, dt_raw, conv_w, conv_b, dt_bias, a_log, d_skip, batch, seq):
    q = SSD_CHUNK
    nchunks = seq // q
    pad = HEAD_PAD - SSD_HEADS
    dtb = jnp.pad(dt_bias, (0, pad)).reshape(1, HEAD_PAD)
    alog = jnp.pad(a_log, (0, pad)).reshape(1, HEAD_PAD)
    head_of_lane = jnp.arange(MAIN_WIDTH) // SSD_HEAD_DIM
    expand = (jnp.arange(HEAD_PAD)[:, None] == head_of_lane[None, :]).astype(BF16)
    dskip = jnp.repeat(d_skip, SSD_HEAD_DIM).reshape(1, MAIN_WIDTH)
    const = lambda shape: pl.BlockSpec(shape, lambda b, c: (0,) * len(shape))
    return pl.pallas_call(
        _ssd_kernel,
        grid=(batch, nchunks),
        in_specs=[pl.BlockSpec((1, q, SSD_CONV_DIM), lambda b, c: (b, c, 0)),
                  pl.BlockSpec((1, q, HEAD_PAD), lambda b, c: (b, c, 0)),
                  const((SSD_CONV, SSD_CONV_DIM)),
                  const((1, SSD_CONV_DIM)),
                  const((1, HEAD_PAD)),
                  const((1, HEAD_PAD)),
                  const((HEAD_PAD, MAIN_WIDTH)),
                  const((1, MAIN_WIDTH))],
        out_specs=pl.BlockSpec((1, q, MAIN_WIDTH), lambda b, c: (b, c, 0)),
        out_shape=jax.ShapeDtypeStruct((batch, seq, MAIN_WIDTH), F32),
        scratch_shapes=[pltpu.VMEM((SUBLANES + q, SSD_CONV_DIM), F32),
                        pltpu.VMEM((q, SSD_CONV_DIM), F32),
                        pltpu.VMEM((3 * q, MAIN_WIDTH), F32),
                        pltpu.VMEM((SSD_STATE, MAIN_WIDTH), F32)],
        compiler_params=_cparams(("parallel", "arbitrary")),
        name="ssd_mixer",
    )(xbc.reshape(batch, seq, SSD_CONV_DIM), dt_raw.reshape(batch, seq, HEAD_PAD),
      conv_w, conv_b.reshape(1, SSD_CONV_DIM), dtb, alog, expand, dskip)


def _sb_kernel(q_ref, k_ref, v_ref, o_ref, *, tile):
    t = tile
    zscale = SB_HEAD_DIM ** -0.5 * LOG2E
    row = lax.broadcasted_iota(jnp.int32, (t, t), 0)
    col = lax.broadcasted_iota(jnp.int32, (t, t), 1)
    strict = row > col
    later = strict.astype(BF16)

    def scores(qb, j):
        kj = k_ref[0, pl.ds(pl.multiple_of(j * t, t), t), :]
        z2 = lax.dot_general(qb, kj, (((1,), (1,)), ((), ())), preferred_element_type=F32) * zscale
        neg_abs = lax.bitcast_convert_type(
            lax.bitcast_convert_type(z2, jnp.uint32) | jnp.uint32(0x80000000), F32)
        log_sig = jnp.minimum(z2, 0.0) - jnp.log2(1.0 + jnp.exp2(neg_abs))
        return log_sig, log_sig - z2

    def finish(j, log_sig, log_keep, acc, o, diagonal):
        vj = v_ref[0, pl.ds(pl.multiple_of(j * t, t), t), :]
        if diagonal:
            log_keep = jnp.where(strict, log_keep, 0.0)
        after = jnp.dot(log_keep.astype(BF16), later, preferred_element_type=F32)
        a = jnp.exp2(log_sig + after + acc)
        if diagonal:
            a = jnp.where(strict, a, 0.0)
        o = o + jnp.dot(a.astype(BF16), vj, preferred_element_type=F32)
        acc = acc + jnp.sum(log_keep, axis=1, keepdims=True)
        return acc, o

    def pair(qb, j_a, j_b, use_b, acc, o, diagonal):
        j_b = jnp.maximum(j_b, 0)
        sig_a, keep_a = scores(qb, j_a)
        sig_b, keep_b = scores(qb, j_b)
        acc, o = finish(j_a, sig_a, keep_a, acc, o, diagonal)
        acc_b, o_b = finish(j_b, sig_b, keep_b, acc, o, False)
        return jnp.where(use_b, acc_b, acc), jnp.where(use_b, o_b, o)

    started = []
    for r in range(SB_SUBBLOCKS):
        i = pl.program_id(2) * SB_SUBBLOCKS + r
        qb = q_ref[0, r * t:(r + 1) * t, :]
        acc, o = pair(qb, i, i - 1, i >= 1, jnp.zeros((t, 1), F32), jnp.zeros((t, SB_HEAD_DIM), F32), True)
        started.append((i, qb, acc, o))

    for r, (i, qb, acc, o) in enumerate(started):
        def unfinished(carry, i=i):
            step, acc, _ = carry
            return jnp.logical_and(step < i // 2, jnp.max(acc) > SB_EXHAUSTED_LOG2)

        def body(carry, i=i, qb=qb):
            step, acc, o = carry
            j_a = i - 2 - 2 * step
            acc, o = pair(qb, j_a, j_a - 1, j_a >= 1, acc, o, False)
            return step + 1, acc, o

        _, _, o = lax.while_loop(unfinished, body, (jnp.int32(0), acc, o))
        o_ref[0, r * t:(r + 1) * t, :] = o


def _sb_mixer(qkv, batch, seq, tile=256):
    rows = SB_SUBBLOCKS * tile
    assert seq % rows == 0
    nq = seq // rows
    qkv3 = qkv.reshape(batch, seq, 3 * MAIN_WIDTH)
    return pl.pallas_call(
        functools.partial(_sb_kernel, tile=tile),
        grid=(batch, SB_HEADS, nq),
        in_specs=[pl.BlockSpec((1, rows, SB_HEAD_DIM), lambda b, h, i: (b, i, h)),
                  pl.BlockSpec((1, seq, SB_HEAD_DIM), lambda b, h, i: (b, 0, SB_HEADS + h)),
                  pl.BlockSpec((1, seq, SB_HEAD_DIM), lambda b, h, i: (b, 0, 2 * SB_HEADS + h))],
        out_specs=pl.BlockSpec((1, rows, SB_HEAD_DIM), lambda b, h, i: (b, i, h)),
        out_shape=jax.ShapeDtypeStruct((batch, seq, MAIN_WIDTH), F32),
        compiler_params=_cparams(("parallel", "parallel", "arbitrary")),
        name="stick_breaking",
    )(qkv3, qkv3, qkv3)


def _mix_kernel(main_ref, z_ref, qm_ref, kv_ref, g_ref, o_ref):
    scale = MEM_HEAD_DIM ** -0.5
    gate_main = _silu(z_ref[:, 0:MAIN_WIDTH])
    ym = main_ref[...] * gate_main
    ss = jnp.sum(ym * ym, axis=-1, keepdims=True)
    parts = []
    for h in range(MEM_HEADS):
        hs = slice(h * MEM_HEAD_DIM, (h + 1) * MEM_HEAD_DIM)
        kh = kv_ref[0, :, hs]
        vh = kv_ref[0, :, MEM_WIDTH + h * MEM_HEAD_DIM:MEM_WIDTH + (h + 1) * MEM_HEAD_DIM]
        s = lax.dot_general(qm_ref[:, hs], kh, (((1,), (1,)), ((), ())),
                            preferred_element_type=F32) * scale
        e = jnp.exp(s - jnp.max(s, axis=-1, keepdims=True))
        p = e * (1.0 / jnp.sum(e, axis=-1, keepdims=True))
        mo = jnp.dot(p.astype(BF16), vh, preferred_element_type=F32)
        zs = slice(MAIN_WIDTH + h * MEM_HEAD_DIM, MAIN_WIDTH + (h + 1) * MEM_HEAD_DIM)
        part = mo * _silu(z_ref[:, zs])
        ss = ss + jnp.sum(part * part, axis=-1, keepdims=True)
        parts.append(part)
    inv = lax.rsqrt(ss * (1.0 / MIX_WIDTH) + EPS)
    o_ref[:, 0:MAIN_WIDTH] = (ym * inv * g_ref[:, 0:MAIN_WIDTH]).astype(o_ref.dtype)
    for h in range(MEM_HEADS):
        zs = slice(MAIN_WIDTH + h * MEM_HEAD_DIM, MAIN_WIDTH + (h + 1) * MEM_HEAD_DIM)
        o_ref[:, zs] = (parts[h] * inv * g_ref[:, zs]).astype(o_ref.dtype)


def _mix(main, z, q_mem, kv, out_norm, seq, tm=256):
    m = main.shape[0]
    tiles_per_batch = seq // tm
    return pl.pallas_call(
        _mix_kernel,
        grid=(m // tm,),
        in_specs=[pl.BlockSpec((tm, MAIN_WIDTH), lambda i: (i, 0)),
                  pl.BlockSpec((tm, MIX_WIDTH), lambda i: (i, 0)),
                  pl.BlockSpec((tm, MEM_WIDTH), lambda i: (i, 0)),
                  pl.BlockSpec((1, MEM_TOKENS, 2 * MEM_WIDTH), lambda i: (i // tiles_per_batch, 0, 0)),
                  pl.BlockSpec((1, MIX_WIDTH), lambda i: (0, 0))],
        out_specs=pl.BlockSpec((tm, MIX_WIDTH), lambda i: (i, 0)),
        out_shape=jax.ShapeDtypeStruct((m, MIX_WIDTH), BF16),
        compiler_params=_cparams(("parallel",)),
        name="memory_gate_norm",
    )(main, z, q_mem, kv, out_norm.reshape(1, MIX_WIDTH))


def kernel(x, mem, mem_norm, ssd_norm, ssd_w_in, ssd_conv_w, ssd_conv_b, ssd_dt_bias, ssd_a_log, ssd_d, ssd_mem_kv, ssd_out_norm, ssd_w_out, sb_norm, sb_w_in, sb_mem_kv, sb_out_norm, sb_w_out, final_norm):
    batch, seq, d = x.shape
    m = batch * seq
    depth = ssd_w_in.shape[0] + sb_w_in.shape[0]
    xf = x.reshape(m, d)
    mem_n = _rmsnorm(mem.reshape(batch * MEM_TOKENS, d), mem_norm, BF16)

    def mixer_tail(xf, main, z, q_mem, w_kv, out_norm, w_out, j):
        kv = _project(mem_n, w_kv, j, 0, 2 * MEM_WIDTH, BF16).reshape(batch, MEM_TOKENS, 2 * MEM_WIDTH)
        yn = _mix(main, z, q_mem, kv, out_norm[j], seq)
        return _matmul(yn, w_out[j].astype(BF16), F32, residual=xf)

    for layer in range(depth):
        j = layer // 2
        if layer % 2 == 0:
            h = _rmsnorm(xf, ssd_norm[j], BF16)
            c0, c1, c2 = SSD_CONV_DIM, SSD_CONV_DIM + SSD_HEADS, SSD_CONV_DIM + SSD_HEADS + MIX_WIDTH
            w_dt = jnp.pad(ssd_w_in[j, :, c0:c1], ((0, 0), (0, HEAD_PAD - SSD_HEADS)))[None]
            xbc = _project(h, ssd_w_in, j, 0, c0, F32)
            dt_raw = _project(h, w_dt, 0, 0, HEAD_PAD, F32)
            z = _project(h, ssd_w_in[j:j + 1, :, c1:c2], 0, 0, MIX_WIDTH, F32)
            q_mem = _project(h, ssd_w_in[j:j + 1, :, c2:], 0, 0, MEM_WIDTH, BF16)
            main = _ssd_mixer(xbc, dt_raw, ssd_conv_w[j], ssd_conv_b[j], ssd_dt_bias[j],
                              ssd_a_log[j], ssd_d[j], batch, seq).reshape(m, MAIN_WIDTH)
            xf = mixer_tail(xf, main, z, q_mem, ssd_mem_kv, ssd_out_norm, ssd_w_out, j)
        else:
            h = _rmsnorm(xf, sb_norm[j], BF16)
            c0, c1 = 3 * MAIN_WIDTH, 3 * MAIN_WIDTH + MIX_WIDTH
            qkv = _project(h, sb_w_in, j, 0, c0, BF16)
            z = _project(h, sb_w_in, j, c0, MIX_WIDTH, F32)
            q_mem = _project(h, sb_w_in, j, c1, MEM_WIDTH, BF16)
            main = _sb_mixer(qkv, batch, seq).reshape(m, MAIN_WIDTH)
            xf = mixer_tail(xf, main, z, q_mem, sb_mem_kv, sb_out_norm, sb_w_out, j)

    return _rmsnorm(xf, final_norm, F32).reshape(batch, seq, d)
```

```python
import functools

import jax
import jax.numpy as jnp
from jax import lax
from jax.experimental import pallas as pl
from jax.experimental.pallas import tpu as pltpu

D_MODEL = 2048
MEM_TOKENS = 256
MIX_WIDTH = 2 * D_MODEL
MEM_HEADS = 4
MEM_WIDTH = MIX_WIDTH // 4
MEM_HEAD_DIM = MEM_WIDTH // MEM_HEADS
MAIN_WIDTH = MIX_WIDTH - MEM_WIDTH
SSD_HEAD_DIM = 64
SSD_HEADS = MAIN_WIDTH // SSD_HEAD_DIM
SSD_GROUPS = 8
SSD_HEADS_PER_GROUP = SSD_HEADS // SSD_GROUPS
SSD_STATE = 128
SSD_CONV = 4
SSD_CHUNK = 128
SSD_BC_WIDTH = SSD_GROUPS * SSD_STATE
SSD_CONV_DIM = MAIN_WIDTH + 2 * SSD_BC_WIDTH
SB_HEAD_DIM = 128
SB_HEADS = MAIN_WIDTH // SB_HEAD_DIM
EPS = 1e-6
LOG2E = 1.4426950408889634
SB_SUBBLOCKS = 4
SB_EXHAUSTED_LOG2 = -160.0

LANES = 128
SUBLANES = 8
HEAD_PAD = LANES
VMEM_LIMIT = 56 * 1024 * 1024

BF16 = jnp.bfloat16
F32 = jnp.float32


def _cparams(semantics):
    return pltpu.CompilerParams(dimension_semantics=semantics, vmem_limit_bytes=VMEM_LIMIT)


def _rms_kernel(x_ref, g_ref, o_ref):
    x = x_ref[...]
    ms = jnp.mean(x * x, axis=-1, keepdims=True)
    o_ref[...] = (x * lax.rsqrt(ms + EPS) * g_ref[...]).astype(o_ref.dtype)


def _rmsnorm(x, g, out_dtype, tm=512):
    m, d = x.shape
    tm = min(tm, m)
    return pl.pallas_call(
        _rms_kernel,
        grid=(m // tm,),
        in_specs=[pl.BlockSpec((tm, d), lambda i: (i, 0)),
                  pl.BlockSpec((1, d), lambda i: (0, 0))],
        out_specs=pl.BlockSpec((tm, d), lambda i: (i, 0)),
        out_shape=jax.ShapeDtypeStruct((m, d), out_dtype),
        compiler_params=_cparams(("parallel",)),
        name="rmsnorm",
    )(x, g.reshape(1, d))


def _inv_rms(sum_sq, width):
    return lax.rsqrt(sum_sq * (1.0 / width) + EPS)


def _prep_kernel(x_ref, g_ref, xg_ref, inv_ref):
    x = x_ref[...]
    xg_ref[...] = (x * g_ref[...]).astype(BF16)
    inv = _inv_rms(jnp.sum(x * x, axis=-1, keepdims=True), x.shape[-1])
    inv_ref[...] = jnp.broadcast_to(inv, inv_ref.shape)


def _prep(x, g, tm=512):
    m, d = x.shape
    return pl.pallas_call(
        _prep_kernel,
        grid=(m // tm,),
        in_specs=[pl.BlockSpec((tm, d), lambda i: (i, 0)),
                  pl.BlockSpec((1, d), lambda i: (0, 0))],
        out_specs=[pl.BlockSpec((tm, d), lambda i: (i, 0)),
                   pl.BlockSpec((tm, LANES), lambda i: (i, 0))],
        out_shape=[jax.ShapeDtypeStruct((m, d), BF16), jax.ShapeDtypeStruct((m, LANES), F32)],
        compiler_params=_cparams(("parallel",)),
        name="norm_prep",
    )(x, g.reshape(1, d))


def _cast_weight_tile(w_ref, wb_ref):
    @pl.when(pl.program_id(1) == 0)
    def _():
        wb_ref[...] = w_ref[...].astype(BF16)


def _proj_kernel(x_ref, w_ref, o_ref, wb_ref):
    _cast_weight_tile(w_ref, wb_ref)
    o_ref[...] = jnp.dot(x_ref[...], wb_ref[...], preferred_element_type=F32).astype(o_ref.dtype)


def _proj_inv_kernel(x_ref, inv_ref, w_ref, o_ref, wb_ref):
    _cast_weight_tile(w_ref, wb_ref)
    acc = jnp.dot(x_ref[...], wb_ref[...], preferred_element_type=F32)
    o_ref[...] = (acc * inv_ref[:, 0:1]).astype(o_ref.dtype)


def _gate_kernel(x_ref, inv_ref, w_ref, other_ref, g_ref, yg_ref, ssq_ref, wb_ref):
    _cast_weight_tile(w_ref, wb_ref)
    z = jnp.dot(x_ref[...], wb_ref[...], preferred_element_type=F32) * inv_ref[:, 0:1]
    y = other_ref[...] * _silu(z)
    yg_ref[...] = (y * g_ref[...]).astype(BF16)
    ssq_ref[...] = jnp.broadcast_to(jnp.sum(y * y, axis=-1, keepdims=True), ssq_ref.shape)


def _project(x, w, layer, col0, n, out_dtype, inv=None, gate=None, tm=1024, tn=1024):
    m, k = x.shape
    tm = min(tm, m)
    tn = min(tn, n)
    assert col0 % tn == 0 and n % tn == 0 and m % tm == 0
    j0 = col0 // tn
    x_spec = pl.BlockSpec((tm, k), lambda j, i: (i, 0))
    inv_spec = pl.BlockSpec((tm, LANES), lambda j, i: (i, 0))
    w_spec = pl.BlockSpec((None, k, tn), lambda j, i: (layer, 0, j0 + j))
    tile_spec = pl.BlockSpec((tm, tn), lambda j, i: (i, j))
    common = dict(grid=(n // tn, m // tm), scratch_shapes=[pltpu.VMEM((k, tn), BF16)],
                  compiler_params=_cparams(("parallel", "arbitrary")))
    if gate is not None:
        other, g = gate
        return pl.pallas_call(
            _gate_kernel,
            in_specs=[x_spec, inv_spec, w_spec, tile_spec, pl.BlockSpec((1, tn), lambda j, i: (0, j))],
            out_specs=[tile_spec, pl.BlockSpec((None, tm, LANES), lambda j, i: (j, i, 0))],
            out_shape=[jax.ShapeDtypeStruct((m, n), BF16),
                       jax.ShapeDtypeStruct((n // tn, m, LANES), F32)],
            name="gate_projection", **common,
        )(x, inv, w, other, g.reshape(1, n))
    if inv is not None:
        return pl.pallas_call(
            _proj_inv_kernel, in_specs=[x_spec, inv_spec, w_spec], out_specs=tile_spec,
            out_shape=jax.ShapeDtypeStruct((m, n), out_dtype), name="projection", **common,
        )(x, inv, w)
    return pl.pallas_call(
        _proj_kernel, in_specs=[x_spec, w_spec], out_specs=tile_spec,
        out_shape=jax.ShapeDtypeStruct((m, n), out_dtype), name="projection_plain", **common,
    )(x, w)


def _out_kernel(yg_ref, ssq_ref, w_ref, res_ref, gn_ref, x_ref, xg_ref, invn_ref, invy_ref, acc_ref):
    j = pl.program_id(1)

    @pl.when(j == 0)
    def _():
        total = ssq_ref[0, :, 0:1]
        for p in range(1, ssq_ref.shape[0]):
            total = total + ssq_ref[p, :, 0:1]
        invy_ref[...] = _inv_rms(total, MIX_WIDTH)
        acc_ref[...] = jnp.zeros_like(acc_ref)

    x_new = res_ref[...] + jnp.dot(yg_ref[...], w_ref[...], preferred_element_type=F32) * invy_ref[...]
    x_ref[...] = x_new
    xg_ref[...] = (x_new * gn_ref[...]).astype(BF16)
    acc_ref[...] += jnp.sum(x_new * x_new, axis=-1, keepdims=True)

    @pl.when(j == pl.num_programs(1) - 1)
    def _():
        invn_ref[...] = jnp.broadcast_to(_inv_rms(acc_ref[...], D_MODEL), invn_ref.shape)


def _out_project(yg, ssq, w, res, g_next, tm=1024, tn=512):
    m, k = yg.shape
    n = w.shape[1]
    parts = ssq.shape[0]
    return pl.pallas_call(
        _out_kernel,
        grid=(m // tm, n // tn),
        in_specs=[pl.BlockSpec((tm, k), lambda i, j: (i, 0)),
                  pl.BlockSpec((parts, tm, LANES), lambda i, j: (0, i, 0)),
                  pl.BlockSpec((k, tn), lambda i, j: (0, j)),
                  pl.BlockSpec((tm, tn), lambda i, j: (i, j)),
                  pl.BlockSpec((1, tn), lambda i, j: (0, j))],
        out_specs=[pl.BlockSpec((tm, tn), lambda i, j: (i, j)),
                   pl.BlockSpec((tm, tn), lambda i, j: (i, j)),
                   pl.BlockSpec((tm, LANES), lambda i, j: (i, 0))],
        out_shape=[jax.ShapeDtypeStruct((m, n), F32), jax.ShapeDtypeStruct((m, n), BF16),
                   jax.ShapeDtypeStruct((m, LANES), F32)],
        scratch_shapes=[pltpu.VMEM((tm, 1), F32), pltpu.VMEM((tm, 1), F32)],
        compiler_params=_cparams(("parallel", "arbitrary")),
        name="out_projection",
    )(yg, ssq, w, res, g_next.reshape(1, n))


def _split3(x):
    p0 = x.astype(BF16)
    r1 = x - p0.astype(F32)
    p1 = r1.astype(BF16)
    p2 = (r1 - p1.astype(F32)).astype(BF16)
    return p0, p1, p2


def _dot_f32_by_01(x, ones_rhs):
    p0, p1, p2 = _split3(x)
    d = functools.partial(jnp.dot, preferred_element_type=F32)
    return d(p0, ones_rhs) + (d(p1, ones_rhs) + d(p2, ones_rhs))


def _dot_01_by_f32(ones_lhs, x):
    p0, p1, p2 = _split3(x)
    d = functools.partial(jnp.dot, preferred_element_type=F32)
    return d(ones_lhs, p0) + (d(ones_lhs, p1) + d(ones_lhs, p2))


def _softplus(x):
    return jnp.maximum(x, 0.0) + jnp.log(1.0 + jnp.exp(-jnp.abs(x)))


def _silu(x):
    return x / (1.0 + jnp.exp(-x))


def _ssd_kernel(xbc_ref, dt_ref, cw_ref, cb_ref, dtb_ref, alog_ref, expand_ref, dskip_ref,
                y_ref, ext_ref, xc_ref, ex_ref, st_ref):
    q = SSD_CHUNK
    hp = SSD_HEAD_DIM
    gw = SSD_HEADS_PER_GROUP * hp

    @pl.when(pl.program_id(1) == 0)
    def _():
        ext_ref[0:SUBLANES, :] = jnp.zeros((SUBLANES, SSD_CONV_DIM), F32)
        st_ref[...] = jnp.zeros_like(st_ref)

    u = xbc_ref[0]
    ext_ref[SUBLANES:SUBLANES + q, :] = u
    conv = cb_ref[...] + cw_ref[SSD_CONV - 1:SSD_CONV, :] * u
    for tap in range(SSD_CONV - 1):
        shift = SSD_CONV - 1 - tap
        conv = conv + cw_ref[tap:tap + 1, :] * ext_ref[SUBLANES - shift:SUBLANES - shift + q, :]
    ext_ref[0:SUBLANES, :] = u[q - SUBLANES:q, :]
    xc_ref[...] = _silu(conv)

    dt = _softplus(dt_ref[0] + dtb_ref[...])
    d_a = dt * (-jnp.exp(alog_ref[...]))
    row = lax.broadcasted_iota(jnp.int32, (q, q), 0)
    col = lax.broadcasted_iota(jnp.int32, (q, q), 1)
    causal = row >= col
    cum = _dot_01_by_f32(causal.astype(BF16), d_a)
    cum_t = cum.T
    cum_end = cum[q - 1:q, :]
    to_end = jnp.exp(cum_end - cum) * dt
    ecum = jnp.exp(cum)

    expand = expand_ref[...]
    ex_ref[0:q, :] = _dot_f32_by_01(dt, expand)
    ex_ref[q:2 * q, :] = _dot_f32_by_01(to_end, expand)
    ex_ref[2 * q:3 * q, :] = _dot_f32_by_01(ecum, expand)

    lane = lax.broadcasted_iota(jnp.int32, (q, LANES), 1)
    low_half = lane < hp

    for g in range(SSD_GROUPS):
        bg = xc_ref[:, MAIN_WIDTH + g * SSD_STATE:MAIN_WIDTH + (g + 1) * SSD_STATE]
        cg = xc_ref[:, MAIN_WIDTH + SSD_BC_WIDTH + g * SSD_STATE:
                    MAIN_WIDTH + SSD_BC_WIDTH + (g + 1) * SSD_STATE]
        cg16 = cg.astype(BF16)
        cb = lax.dot_general(cg16, bg.astype(BF16), (((1,), (1,)), ((), ())),
                             preferred_element_type=F32)
        gs = slice(g * gw, (g + 1) * gw)
        y_off = jnp.dot(cg16, st_ref[:, gs].astype(BF16), preferred_element_type=F32)

        for pair in range(SSD_HEADS_PER_GROUP // 2):
            ws = []
            for r in (2 * pair, 2 * pair + 1):
                h = g * SSD_HEADS_PER_GROUP + r
                seg = cum[:, h:h + 1] - cum_t[h:h + 1, :]
                ws.append(cb * jnp.exp(jnp.where(causal, seg, -jnp.inf)))
            w2 = jnp.concatenate(ws, axis=1).astype(BF16)
            ps = slice(g * gw + pair * LANES, g * gw + (pair + 1) * LANES)
            xs = xc_ref[:, ps]
            xdt = xs * ex_ref[0:q, ps]
            rhs = jnp.concatenate([jnp.where(low_half, xdt, 0.0),
                                   jnp.where(low_half, 0.0, xdt)], axis=0).astype(BF16)
            y_diag = jnp.dot(w2, rhs, preferred_element_type=F32)
            yo = y_off[:, pair * LANES:(pair + 1) * LANES]
            y_ref[0, :, ps] = y_diag + ex_ref[2 * q:3 * q, ps] * yo + xs * dskip_ref[:, ps]

        xte = (xc_ref[:, gs] * ex_ref[q:2 * q, gs]).astype(BF16)
        upd = jnp.dot(bg.T.astype(BF16), xte, preferred_element_type=F32)
        st_ref[:, gs] = st_ref[:, gs] * ex_ref[3 * q - 1:3 * q, gs] + upd


def _ssd_mixer(xbc, dt_raw, conv_w, conv_b, dt_bias, a_log, d_skip, batch, seq):
    q = SSD_CHUNK
    nchunks = seq // q
    pad = HEAD_PAD - SSD_HEADS
    dtb = jnp.pad(dt_bias, (0, pad)).reshape(1, HEAD_PAD)
    alog = jnp.pad(a_log, (0, pad)).reshape(1, HEAD_PAD)
    head_of_lane = jnp.arange(MAIN_WIDTH) // SSD_HEAD_DIM
    expand = (jnp.arange(HEAD_PAD)[:, None] == head_of_lane[None, :]).astype(BF16)
    dskip = jnp.repeat(d_skip, SSD_HEAD_DIM).reshape(1, MAIN_WIDTH)
    const = lambda shape: pl.BlockSpec(shape, lambda b, c: (0,) * len(shape))
    return pl.pallas_call(
        _ssd_kernel,
        grid=(batch, nchunks),
        in_specs=[pl.BlockSpec((1, q, SSD_CONV_DIM), lambda b, c: (b, c, 0)),
                  pl.BlockSpec((1, q, HEAD_PAD), lambda b, c: (b, c, 0)),
                  const((SSD_CONV, SSD_CONV_DIM)),
                  const((1, SSD_CONV_DIM)),
                  const((1, HEAD_PAD)),
                  const((1, HEAD_PAD)),
                  const((HEAD_PAD, MAIN_WIDTH)),
                  const((1, MAIN_WIDTH))],
        out_specs=pl.BlockSpec((1, q, MAIN_WIDTH), lambda b, c: (b, c, 0)),
        out_shape=jax.ShapeDtypeStruct((batch, seq, MIX_WIDTH), F32),
        scratch_shapes=[pltpu.VMEM((SUBLANES + q, SSD_CONV_DIM), F32),
                        pltpu.VMEM((q, SSD_CONV_DIM), F32),
                        pltpu.VMEM((3 * q, MAIN_WIDTH), F32),
                        pltpu.VMEM((SSD_STATE, MAIN_WIDTH), F32)],
        compiler_params=_cparams(("parallel", "arbitrary")),
        name="ssd_mixer",
    )(xbc.reshape(batch, seq, SSD_CONV_DIM), dt_raw.reshape(batch, seq, HEAD_PAD),
      conv_w, conv_b.reshape(1, SSD_CONV_DIM), dtb, alog, expand, dskip)


def _sb_kernel(q_ref, k_ref, v_ref, o_ref, *, tile):
    t = tile
    zscale = SB_HEAD_DIM ** -0.5 * LOG2E
    row = lax.broadcasted_iota(jnp.int32, (t, t), 0)
    col = lax.broadcasted_iota(jnp.int32, (t, t), 1)
    strict = row > col
    later = strict.astype(BF16)

    def scores(qb, j):
        kj = k_ref[0, pl.ds(pl.multiple_of(j * t, t), t), :]
        z2 = lax.dot_general(qb, kj, (((1,), (1,)), ((), ())), preferred_element_type=F32) * zscale
        neg_abs = lax.bitcast_convert_type(
            lax.bitcast_convert_type(z2, jnp.uint32) | jnp.uint32(0x80000000), F32)
        log_sig = jnp.minimum(z2, 0.0) - jnp.log2(1.0 + jnp.exp2(neg_abs))
        return log_sig, log_sig - z2

    def finish(j, log_sig, log_keep, acc, o, diagonal):
        vj = v_ref[0, pl.ds(pl.multiple_of(j * t, t), t), :]
        if diagonal:
            log_keep = jnp.where(strict, log_keep, 0.0)
        after = jnp.dot(log_keep.astype(BF16), later, preferred_element_type=F32)
        a = jnp.exp2(log_sig + after + acc)
        if diagonal:
            a = jnp.where(strict, a, 0.0)
        o = o + jnp.dot(a.astype(BF16), vj, preferred_element_type=F32)
        acc = acc + jnp.sum(log_keep, axis=1, keepdims=True)
        return acc, o

    def pair(qb, j_a, j_b, use_b, acc, o, diagonal):
        j_b = jnp.maximum(j_b, 0)
        sig_a, keep_a = scores(qb, j_a)
        sig_b, keep_b = scores(qb, j_b)
        acc, o = finish(j_a, sig_a, keep_a, acc, o, diagonal)
        acc_b, o_b = finish(j_b, sig_b, keep_b, acc, o, False)
        return jnp.where(use_b, acc_b, acc), jnp.where(use_b, o_b, o)

    started = []
    for r in range(SB_SUBBLOCKS):
        i = pl.program_id(2) * SB_SUBBLOCKS + r
        qb = q_ref[0, r * t:(r + 1) * t, :]
        acc, o = pair(qb, i, i - 1, i >= 1, jnp.zeros((t, 1), F32), jnp.zeros((t, SB_HEAD_DIM), F32), True)
        started.append((i, qb, acc, o))

    for r, (i, qb, acc, o) in enumerate(started):
        def unfinished(carry, i=i):
            step, acc, _ = carry
            return jnp.logical_and(step < i // 2, jnp.max(acc) > SB_EXHAUSTED_LOG2)

        def body(carry, i=i, qb=qb):
            step, acc, o = carry
            j_a = i - 2 - 2 * step
            acc, o = pair(qb, j_a, j_a - 1, j_a >= 1, acc, o, False)
            return step + 1, acc, o

        _, _, o = lax.while_loop(unfinished, body, (jnp.int32(0), acc, o))
        o_ref[0, r * t:(r + 1) * t, :] = o


def _sb_mixer(qkv, batch, seq, tile=256):
    rows = SB_SUBBLOCKS * tile
    assert seq % rows == 0
    nq = seq // rows
    qkv3 = qkv.reshape(batch, seq, 3 * MAIN_WIDTH)
    return pl.pallas_call(
        functools.partial(_sb_kernel, tile=tile),
        grid=(batch, SB_HEADS, nq),
        in_specs=[pl.BlockSpec((1, rows, SB_HEAD_DIM), lambda b, h, i: (b, i, h)),
                  pl.BlockSpec((1, seq, SB_HEAD_DIM), lambda b, h, i: (b, 0, SB_HEADS + h)),
                  pl.BlockSpec((1, seq, SB_HEAD_DIM), lambda b, h, i: (b, 0, 2 * SB_HEADS + h))],
        out_specs=pl.BlockSpec((1, rows, SB_HEAD_DIM), lambda b, h, i: (b, i, h)),
        out_shape=jax.ShapeDtypeStruct((batch, seq, MIX_WIDTH), F32),
        compiler_params=_cparams(("parallel", "parallel", "arbitrary")),
        name="stick_breaking",
    )(qkv3, qkv3, qkv3)


def _mem_kernel(qm_ref, kv_ref, buf_ref, o_ref):
    del buf_ref
    scale = MEM_HEAD_DIM ** -0.5
    for h in range(MEM_HEADS):
        hs = slice(h * MEM_HEAD_DIM, (h + 1) * MEM_HEAD_DIM)
        kh = kv_ref[0, :, hs]
        vh = kv_ref[0, :, MEM_WIDTH + h * MEM_HEAD_DIM:MEM_WIDTH + (h + 1) * MEM_HEAD_DIM]
        s = lax.dot_general(qm_ref[:, hs], kh, (((1,), (1,)), ((), ())),
                            preferred_element_type=F32) * scale
        e = jnp.exp(s - jnp.max(s, axis=-1, keepdims=True))
        p = e * (1.0 / jnp.sum(e, axis=-1, keepdims=True))
        o_ref[:, hs] = jnp.dot(p.astype(BF16), vh, preferred_element_type=F32)


def _mem_attention(q_mem, kv, buf, seq, tm=512):
    m = q_mem.shape[0]
    tiles_per_batch = seq // tm
    return pl.pallas_call(
        _mem_kernel,
        grid=(m // tm,),
        in_specs=[pl.BlockSpec((tm, MEM_WIDTH), lambda i: (i, 0)),
                  pl.BlockSpec((1, MEM_TOKENS, 2 * MEM_WIDTH), lambda i: (i // tiles_per_batch, 0, 0)),
                  pl.BlockSpec(memory_space=pl.ANY)],
        out_specs=pl.BlockSpec((tm, MEM_WIDTH), lambda i: (i, MAIN_WIDTH // MEM_WIDTH)),
        out_shape=jax.ShapeDtypeStruct(buf.shape, buf.dtype),
        input_output_aliases={2: 0},
        compiler_params=_cparams(("parallel",)),
        name="memory_attention",
    )(q_mem, kv, buf)


def kernel(x, mem, mem_norm, ssd_norm, ssd_w_in, ssd_conv_w, ssd_conv_b, ssd_dt_bias, ssd_a_log, ssd_d, ssd_mem_kv, ssd_out_norm, ssd_w_out, sb_norm, sb_w_in, sb_mem_kv, sb_out_norm, sb_w_out, final_norm):
    batch, seq, d = x.shape
    m = batch * seq
    depth = ssd_w_in.shape[0] + sb_w_in.shape[0]
    xf = x.reshape(m, d)
    mem_n = _rmsnorm(mem.reshape(batch * MEM_TOKENS, d), mem_norm, BF16)
    in_norm = lambda layer: (ssd_norm if layer % 2 == 0 else sb_norm)[layer // 2]

    xg, inv = _prep(xf, in_norm(0))
    for layer in range(depth):
        j = layer // 2
        if layer % 2 == 0:
            w_in, w_kv, out_norm, w_out = ssd_w_in, ssd_mem_kv, ssd_out_norm, ssd_w_out
            c0, c1, c2 = SSD_CONV_DIM, SSD_CONV_DIM + SSD_HEADS, SSD_CONV_DIM + SSD_HEADS + MIX_WIDTH
            w_dt = jnp.pad(w_in[j, :, c0:c1], ((0, 0), (0, HEAD_PAD - SSD_HEADS)))[None]
            w_z, lz, z0 = w_in[j:j + 1, :, c1:c2], 0, 0
            w_q, lq, q0 = w_in[j:j + 1, :, c2:], 0, 0
            xbc = _project(xg, w_in, j, 0, c0, F32, inv=inv)
            dt_raw = _project(xg, w_dt, 0, 0, HEAD_PAD, F32, inv=inv)
            buf = _ssd_mixer(xbc, dt_raw, ssd_conv_w[j], ssd_conv_b[j], ssd_dt_bias[j],
                             ssd_a_log[j], ssd_d[j], batch, seq)
        else:
            w_in, w_kv, out_norm, w_out = sb_w_in, sb_mem_kv, sb_out_norm, sb_w_out
            c0, c1 = 3 * MAIN_WIDTH, 3 * MAIN_WIDTH + MIX_WIDTH
            w_z, lz, z0 = w_in, j, c0
            w_q, lq, q0 = w_in, j, c1
            qkv = _project(xg, w_in, j, 0, c0, BF16, inv=inv)
            buf = _sb_mixer(qkv, batch, seq)
        q_mem = _project(xg, w_q, lq, q0, MEM_WIDTH, BF16, inv=inv)
        kv = _project(mem_n, w_kv, j, 0, 2 * MEM_WIDTH, BF16).reshape(batch, MEM_TOKENS, 2 * MEM_WIDTH)
        buf = _mem_attention(q_mem, kv, buf.reshape(m, MIX_WIDTH), seq)
        yg, ssq = _project(xg, w_z, lz, z0, MIX_WIDTH, BF16, inv=inv, gate=(buf, out_norm[j]))
        g_next = in_norm(layer + 1) if layer + 1 < depth else final_norm
        xf, xg, inv = _out_project(yg, ssq, w_out[j].astype(BF16), xf, g_next)

    return _rmsnorm(xf, final_norm, F32).reshape(batch, seq, d)
```

```python
import functools

import jax
import jax.numpy as jnp
from jax import lax
from jax.experimental import pallas as pl
from jax.experimental.pallas import tpu as pltpu

D_MODEL = 2048
MEM_TOKENS = 256
MIX_WIDTH = 2 * D_MODEL
MEM_HEADS = 4
MEM_WIDTH = MIX_WIDTH // 4
MEM_HEAD_DIM = MEM_WIDTH // MEM_HEADS
MAIN_WIDTH = MIX_WIDTH - MEM_WIDTH
SSD_HEAD_DIM = 64
SSD_HEADS = MAIN_WIDTH // SSD_HEAD_DIM
SSD_GROUPS = 8
SSD_HEADS_PER_GROUP = SSD_HEADS // SSD_GROUPS
SSD_STATE = 128
SSD_CONV = 4
SSD_CHUNK = 128
SSD_BC_WIDTH = SSD_GROUPS * SSD_STATE
SSD_CONV_DIM = MAIN_WIDTH + 2 * SSD_BC_WIDTH
SB_HEAD_DIM = 128
SB_HEADS = MAIN_WIDTH // SB_HEAD_DIM
EPS = 1e-6
LOG2E = 1.4426950408889634
SB_SUBBLOCKS = 4

LANES = 128
SUBLANES = 8
HEAD_PAD = LANES
VMEM_LIMIT = 56 * 1024 * 1024

BF16 = jnp.bfloat16
F32 = jnp.float32


def _cparams(semantics):
    return pltpu.CompilerParams(dimension_semantics=semantics, vmem_limit_bytes=VMEM_LIMIT)


def _rms_kernel(x_ref, g_ref, o_ref):
    x = x_ref[...]
    ms = jnp.mean(x * x, axis=-1, keepdims=True)
    o_ref[...] = (x * lax.rsqrt(ms + EPS) * g_ref[...]).astype(o_ref.dtype)


def _rmsnorm(x, g, out_dtype, tm=512):
    m, d = x.shape
    tm = min(tm, m)
    return pl.pallas_call(
        _rms_kernel,
        grid=(m // tm,),
        in_specs=[pl.BlockSpec((tm, d), lambda i: (i, 0)),
                  pl.BlockSpec((1, d), lambda i: (0, 0))],
        out_specs=pl.BlockSpec((tm, d), lambda i: (i, 0)),
        out_shape=jax.ShapeDtypeStruct((m, d), out_dtype),
        compiler_params=_cparams(("parallel",)),
        name="rmsnorm",
    )(x, g.reshape(1, d))


def _inv_rms(sum_sq, width):
    return lax.rsqrt(sum_sq * (1.0 / width) + EPS)


def _prep_kernel(x_ref, g_ref, xg_ref, inv_ref):
    x = x_ref[...]
    xg_ref[...] = (x * g_ref[...]).astype(BF16)
    inv = _inv_rms(jnp.sum(x * x, axis=-1, keepdims=True), x.shape[-1])
    inv_ref[...] = jnp.broadcast_to(inv, inv_ref.shape)


def _prep(x, g, tm=512):
    m, d = x.shape
    return pl.pallas_call(
        _prep_kernel,
        grid=(m // tm,),
        in_specs=[pl.BlockSpec((tm, d), lambda i: (i, 0)),
                  pl.BlockSpec((1, d), lambda i: (0, 0))],
        out_specs=[pl.BlockSpec((tm, d), lambda i: (i, 0)),
                   pl.BlockSpec((tm, LANES), lambda i: (i, 0))],
        out_shape=[jax.ShapeDtypeStruct((m, d), BF16), jax.ShapeDtypeStruct((m, LANES), F32)],
        compiler_params=_cparams(("parallel",)),
        name="norm_prep",
    )(x, g.reshape(1, d))


def _cast_weight_tile(w_ref, wb_ref):
    @pl.when(pl.program_id(1) == 0)
    def _():
        wb_ref[...] = w_ref[...].astype(BF16)


def _proj_kernel(x_ref, w_ref, o_ref, wb_ref):
    _cast_weight_tile(w_ref, wb_ref)
    o_ref[...] = jnp.dot(x_ref[...], wb_ref[...], preferred_element_type=F32).astype(o_ref.dtype)


def _proj_inv_kernel(x_ref, inv_ref, w_ref, o_ref, wb_ref):
    _cast_weight_tile(w_ref, wb_ref)
    acc = jnp.dot(x_ref[...], wb_ref[...], preferred_element_type=F32)
    o_ref[...] = (acc * inv_ref[:, 0:1]).astype(o_ref.dtype)


def _gate_kernel(x_ref, inv_ref, w_ref, other_ref, g_ref, yg_ref, ssq_ref, wb_ref):
    _cast_weight_tile(w_ref, wb_ref)
    z = jnp.dot(x_ref[...], wb_ref[...], preferred_element_type=F32) * inv_ref[:, 0:1]
    y = other_ref[...] * _silu(z)
    yg_ref[...] = (y * g_ref[...]).astype(BF16)
    ssq_ref[...] = jnp.broadcast_to(jnp.sum(y * y, axis=-1, keepdims=True), ssq_ref.shape)


def _project(x, w, layer, col0, n, out_dtype, inv=None, gate=None, tm=1024, tn=1024):
    m, k = x.shape
    tm = min(tm, m)
    tn = min(tn, n)
    assert col0 % tn == 0 and n % tn == 0 and m % tm == 0
    j0 = col0 // tn
    x_spec = pl.BlockSpec((tm, k), lambda j, i: (i, 0))
    inv_spec = pl.BlockSpec((tm, LANES), lambda j, i: (i, 0))
    w_spec = pl.BlockSpec((None, k, tn), lambda j, i: (layer, 0, j0 + j))
    tile_spec = pl.BlockSpec((tm, tn), lambda j, i: (i, j))
    common = dict(grid=(n // tn, m // tm), scratch_shapes=[pltpu.VMEM((k, tn), BF16)],
                  compiler_params=_cparams(("parallel", "arbitrary")))
    if gate is not None:
        other, g = gate
        return pl.pallas_call(
            _gate_kernel,
            in_specs=[x_spec, inv_spec, w_spec, tile_spec, pl.BlockSpec((1, tn), lambda j, i: (0, j))],
            out_specs=[tile_spec, pl.BlockSpec((None, tm, LANES), lambda j, i: (j, i, 0))],
            out_shape=[jax.ShapeDtypeStruct((m, n), BF16),
                       jax.ShapeDtypeStruct((n // tn, m, LANES), F32)],
            name="gate_projection", **common,
        )(x, inv, w, other, g.reshape(1, n))
    if inv is not None:
        return pl.pallas_call(
            _proj_inv_kernel, in_specs=[x_spec, inv_spec, w_spec], out_specs=tile_spec,
            out_shape=jax.ShapeDtypeStruct((m, n), out_dtype), name="projection", **common,
        )(x, inv, w)
    return pl.pallas_call(
        _proj_kernel, in_specs=[x_spec, w_spec], out_specs=tile_spec,
        out_shape=jax.ShapeDtypeStruct((m, n), out_dtype), name="projection_plain", **common,
    )(x, w)


def _out_kernel(yg_ref, ssq_ref, w_ref, res_ref, gn_ref, x_ref, xg_ref, invn_ref, invy_ref, acc_ref):
    j = pl.program_id(1)

    @pl.when(j == 0)
    def _():
        total = ssq_ref[0, :, 0:1]
        for p in range(1, ssq_ref.shape[0]):
            total = total + ssq_ref[p, :, 0:1]
        invy_ref[...] = _inv_rms(total, MIX_WIDTH)
        acc_ref[...] = jnp.zeros_like(acc_ref)

    x_new = res_ref[...] + jnp.dot(yg_ref[...], w_ref[...], preferred_element_type=F32) * invy_ref[...]
    x_ref[...] = x_new
    xg_ref[...] = (x_new * gn_ref[...]).astype(BF16)
    acc_ref[...] += jnp.sum(x_new * x_new, axis=-1, keepdims=True)

    @pl.when(j == pl.num_programs(1) - 1)
    def _():
        invn_ref[...] = jnp.broadcast_to(_inv_rms(acc_ref[...], D_MODEL), invn_ref.shape)


def _out_project(yg, ssq, w, res, g_next, tm=1024, tn=512):
    m, k = yg.shape
    n = w.shape[1]
    parts = ssq.shape[0]
    return pl.pallas_call(
        _out_kernel,
        grid=(m // tm, n // tn),
        in_specs=[pl.BlockSpec((tm, k), lambda i, j: (i, 0)),
                  pl.BlockSpec((parts, tm, LANES), lambda i, j: (0, i, 0)),
                  pl.BlockSpec((k, tn), lambda i, j: (0, j)),
                  pl.BlockSpec((tm, tn), lambda i, j: (i, j)),
                  pl.BlockSpec((1, tn), lambda i, j: (0, j))],
        out_specs=[pl.BlockSpec((tm, tn), lambda i, j: (i, j)),
                   pl.BlockSpec((tm, tn), lambda i, j: (i, j)),
                   pl.BlockSpec((tm, LANES), lambda i, j: (i, 0))],
        out_shape=[jax.ShapeDtypeStruct((m, n), F32), jax.ShapeDtypeStruct((m, n), BF16),
                   jax.ShapeDtypeStruct((m, LANES), F32)],
        scratch_shapes=[pltpu.VMEM((tm, 1), F32), pltpu.VMEM((tm, 1), F32)],
        compiler_params=_cparams(("parallel", "arbitrary")),
        name="out_projection",
    )(yg, ssq, w, res, g_next.reshape(1, n))


def _split3(x):
    p0 = x.astype(BF16)
    r1 = x - p0.astype(F32)
    p1 = r1.astype(BF16)
    p2 = (r1 - p1.astype(F32)).astype(BF16)
    return p0, p1, p2


def _dot_f32_by_01(x, ones_rhs):
    p0, p1, p2 = _split3(x)
    d = functools.partial(jnp.dot, preferred_element_type=F32)
    return d(p0, ones_rhs) + (d(p1, ones_rhs) + d(p2, ones_rhs))


def _dot_01_by_f32(ones_lhs, x):
    p0, p1, p2 = _split3(x)
    d = functools.partial(jnp.dot, preferred_element_type=F32)
    return d(ones_lhs, p0) + (d(ones_lhs, p1) + d(ones_lhs, p2))


def _softplus(x):
    return jnp.maximum(x, 0.0) + jnp.log(1.0 + jnp.exp(-jnp.abs(x)))


def _silu(x):
    return x / (1.0 + jnp.exp(-x))


def _ssd_kernel(xbc_ref, dt_ref, cw_ref, cb_ref, dtb_ref, alog_ref, expand_ref, dskip_ref,
                y_ref, ext_ref, xc_ref, ex_ref, st_ref):
    q = SSD_CHUNK
    hp = SSD_HEAD_DIM
    gw = SSD_HEADS_PER_GROUP * hp

    @pl.when(pl.program_id(1) == 0)
    def _():
        ext_ref[0:SUBLANES, :] = jnp.zeros((SUBLANES, SSD_CONV_DIM), F32)
        st_ref[...] = jnp.zeros_like(st_ref)

    u = xbc_ref[0]
    ext_ref[SUBLANES:SUBLANES + q, :] = u
    conv = cb_ref[...] + cw_ref[SSD_CONV - 1:SSD_CONV, :] * u
    for tap in range(SSD_CONV - 1):
        shift = SSD_CONV - 1 - tap
        conv = conv + cw_ref[tap:tap + 1, :] * ext_ref[SUBLANES - shift:SUBLANES - shift + q, :]
    ext_ref[0:SUBLANES, :] = u[q - SUBLANES:q, :]
    xc_ref[...] = _silu(conv)

    dt = _softplus(dt_ref[0] + dtb_ref[...])
    d_a = dt * (-jnp.exp(alog_ref[...]))
    row = lax.broadcasted_iota(jnp.int32, (q, q), 0)
    col = lax.broadcasted_iota(jnp.int32, (q, q), 1)
    causal = row >= col
    cum = _dot_01_by_f32(causal.astype(BF16), d_a)
    cum_t = cum.T
    cum_end = cum[q - 1:q, :]
    to_end = jnp.exp(cum_end - cum) * dt
    ecum = jnp.exp(cum)

    expand = expand_ref[...]
    ex_ref[0:q, :] = _dot_f32_by_01(dt, expand)
    ex_ref[q:2 * q, :] = _dot_f32_by_01(to_end, expand)
    ex_ref[2 * q:3 * q, :] = _dot_f32_by_01(ecum, expand)

    lane = lax.broadcasted_iota(jnp.int32, (q, LANES), 1)
    low_half = lane < hp

    for g in range(SSD_GROUPS):
        bg = xc_ref[:, MAIN_WIDTH + g * SSD_STATE:MAIN_WIDTH + (g + 1) * SSD_STATE]
        cg = xc_ref[:, MAIN_WIDTH + SSD_BC_WIDTH + g * SSD_STATE:
                    MAIN_WIDTH + SSD_BC_WIDTH + (g + 1) * SSD_STATE]
        cg16 = cg.astype(BF16)
        cb = lax.dot_general(cg16, bg.astype(BF16), (((1,), (1,)), ((), ())),
                             preferred_element_type=F32)
        gs = slice(g * gw, (g + 1) * gw)
        y_off = jnp.dot(cg16, st_ref[:, gs].astype(BF16), preferred_element_type=F32)

        for pair in range(SSD_HEADS_PER_GROUP // 2):
            ws = []
            for r in (2 * pair, 2 * pair + 1):
                h = g * SSD_HEADS_PER_GROUP + r
                seg = cum[:, h:h + 1] - cum_t[h:h + 1, :]
                ws.append(cb * jnp.exp(jnp.where(causal, seg, -jnp.inf)))
            w2 = jnp.concatenate(ws, axis=1).astype(BF16)
            ps = slice(g * gw + pair * LANES, g * gw + (pair + 1) * LANES)
            xs = xc_ref[:, ps]
            xdt = xs * ex_ref[0:q, ps]
            rhs = jnp.concatenate([jnp.where(low_half, xdt, 0.0),
                                   jnp.where(low_half, 0.0, xdt)], axis=0).astype(BF16)
            y_diag = jnp.dot(w2, rhs, preferred_element_type=F32)
            yo = y_off[:, pair * LANES:(pair + 1) * LANES]
            y_ref[0, :, ps] = y_diag + ex_ref[2 * q:3 * q, ps] * yo + xs * dskip_ref[:, ps]

        xte = (xc_ref[:, gs] * ex_ref[q:2 * q, gs]).astype(BF16)
        upd = jnp.dot(bg.T.astype(BF16), xte, preferred_element_type=F32)
        st_ref[:, gs] = st_ref[:, gs] * ex_ref[3 * q - 1:3 * q, gs] + upd


def _ssd_mixer(xbc, dt_raw, conv_w, conv_b, dt_bias, a_log, d_skip, batch, seq):
    q = SSD_CHUNK
    nchunks = seq // q
    pad = HEAD_PAD - SSD_HEADS
    dtb = jnp.pad(dt_bias, (0, pad)).reshape(1, HEAD_PAD)
    alog = jnp.pad(a_log, (0, pad)).reshape(1, HEAD_PAD)
    head_of_lane = jnp.arange(MAIN_WIDTH) // SSD_HEAD_DIM
    expand = (jnp.arange(HEAD_PAD)[:, None] == head_of_lane[None, :]).astype(BF16)
    dskip = jnp.repeat(d_skip, SSD_HEAD_DIM).reshape(1, MAIN_WIDTH)
    const = lambda shape: pl.BlockSpec(shape, lambda b, c: (0,) * len(shape))
    return pl.pallas_call(
        _ssd_kernel,
        grid=(batch, nchunks),
        in_specs=[pl.BlockSpec((1, q, SSD_CONV_DIM), lambda b, c: (b, c, 0)),
                  pl.BlockSpec((1, q, HEAD_PAD), lambda b, c: (b, c, 0)),
                  const((SSD_CONV, SSD_CONV_DIM)),
                  const((1, SSD_CONV_DIM)),
                  const((1, HEAD_PAD)),
                  const((1, HEAD_PAD)),
                  const((HEAD_PAD, MAIN_WIDTH)),
                  const((1, MAIN_WIDTH))],
        out_specs=pl.BlockSpec((1, q, MAIN_WIDTH), lambda b, c: (b, c, 0)),
        out_shape=jax.ShapeDtypeStruct((batch, seq, MIX_WIDTH), F32),
        scratch_shapes=[pltpu.VMEM((SUBLANES + q, SSD_CONV_DIM), F32),
                        pltpu.VMEM((q, SSD_CONV_DIM), F32),
                        pltpu.VMEM((3 * q, MAIN_WIDTH), F32),
                        pltpu.VMEM((SSD_STATE, MAIN_WIDTH), F32)],
        compiler_params=_cparams(("parallel", "arbitrary")),
        name="ssd_mixer",
    )(xbc.reshape(batch, seq, SSD_CONV_DIM), dt_raw.reshape(batch, seq, HEAD_PAD),
      conv_w, conv_b.reshape(1, SSD_CONV_DIM), dtb, alog, expand, dskip)


def _sb_kernel(q_ref, k_ref, v_ref, o_ref, *, tile):
    t = tile
    zscale = SB_HEAD_DIM ** -0.5 * LOG2E
    row = lax.broadcasted_iota(jnp.int32, (t, t), 0)
    col = lax.broadcasted_iota(jnp.int32, (t, t), 1)
    strict = row > col
    later = strict.astype(BF16)

    def scores(qb, j):
        kj = k_ref[0, pl.ds(pl.multiple_of(j * t, t), t), :]
        z2 = lax.dot_general(qb, kj, (((1,), (1,)), ((), ())), preferred_element_type=F32) * zscale
        neg_abs = lax.bitcast_convert_type(
            lax.bitcast_convert_type(z2, jnp.uint32) | jnp.uint32(0x80000000), F32)
        log_sig = jnp.minimum(z2, 0.0) - jnp.log2(1.0 + jnp.exp2(neg_abs))
        return log_sig, log_sig - z2

    def finish(j, log_sig, log_keep, acc, o, diagonal):
        vj = v_ref[0, pl.ds(pl.multiple_of(j * t, t), t), :]
        if diagonal:
            log_keep = jnp.where(strict, log_keep, 0.0)
        after = jnp.dot(log_keep.astype(BF16), later, preferred_element_type=F32)
        a = jnp.exp2(log_sig + after + acc)
        if diagonal:
            a = jnp.where(strict, a, 0.0)
        o = o + jnp.dot(a.astype(BF16), vj, preferred_element_type=F32)
        acc = acc + jnp.sum(log_keep, axis=1, keepdims=True)
        return acc, o

    def pair(qb, j_a, j_b, use_b, acc, o, diagonal):
        j_b = jnp.maximum(j_b, 0)
        sig_a, keep_a = scores(qb, j_a)
        sig_b, keep_b = scores(qb, j_b)
        acc, o = finish(j_a, sig_a, keep_a, acc, o, diagonal)
        acc_b, o_b = finish(j_b, sig_b, keep_b, acc, o, False)
        return jnp.where(use_b, acc_b, acc), jnp.where(use_b, o_b, o)

    started = []
    for r in range(SB_SUBBLOCKS):
        i = pl.program_id(2) * SB_SUBBLOCKS + r
        qb = q_ref[0, r * t:(r + 1) * t, :]
        acc, o = pair(qb, i, i - 1, i >= 1, jnp.zeros((t, 1), F32), jnp.zeros((t, SB_HEAD_DIM), F32), True)
        started.append((i, qb, acc, o))

    for r, (i, qb, acc, o) in enumerate(started):
        def unfinished(carry, i=i):
            step, acc, _ = carry
            return step < i // 2

        def body(carry, i=i, qb=qb):
            step, acc, o = carry
            j_a = i - 2 - 2 * step
            acc, o = pair(qb, j_a, j_a - 1, j_a >= 1, acc, o, False)
            return step + 1, acc, o

        _, _, o = lax.while_loop(unfinished, body, (jnp.int32(0), acc, o))
        o_ref[0, r * t:(r + 1) * t, :] = o


def _sb_mixer(qkv, batch, seq, tile=256):
    rows = SB_SUBBLOCKS * tile
    assert seq % rows == 0
    nq = seq // rows
    qkv3 = qkv.reshape(batch, seq, 3 * MAIN_WIDTH)
    return pl.pallas_call(
        functools.partial(_sb_kernel, tile=tile),
        grid=(batch, SB_HEADS, nq),
        in_specs=[pl.BlockSpec((1, rows, SB_HEAD_DIM), lambda b, h, i: (b, i, h)),
                  pl.BlockSpec((1, seq, SB_HEAD_DIM), lambda b, h, i: (b, 0, SB_HEADS + h)),
                  pl.BlockSpec((1, seq, SB_HEAD_DIM), lambda b, h, i: (b, 0, 2 * SB_HEADS + h))],
        out_specs=pl.BlockSpec((1, rows, SB_HEAD_DIM), lambda b, h, i: (b, i, h)),
        out_shape=jax.ShapeDtypeStruct((batch, seq, MIX_WIDTH), F32),
        compiler_params=_cparams(("parallel", "parallel", "arbitrary")),
        name="stick_breaking",
    )(qkv3, qkv3, qkv3)


def _mem_kernel(qm_ref, kv_ref, buf_ref, o_ref):
    del buf_ref
    scale = MEM_HEAD_DIM ** -0.5
    for h in range(MEM_HEADS):
        hs = slice(h * MEM_HEAD_DIM, (h + 1) * MEM_HEAD_DIM)
        kh = kv_ref[0, :, hs]
        vh = kv_ref[0, :, MEM_WIDTH + h * MEM_HEAD_DIM:MEM_WIDTH + (h + 1) * MEM_HEAD_DIM]
        s = lax.dot_general(qm_ref[:, hs], kh, (((1,), (1,)), ((), ())),
                            preferred_element_type=F32) * scale
        e = jnp.exp(s - jnp.max(s, axis=-1, keepdims=True))
        p = e * (1.0 / jnp.sum(e, axis=-1, keepdims=True))
        o_ref[:, hs] = jnp.dot(p.astype(BF16), vh, preferred_element_type=F32)


def _mem_attention(q_mem, kv, buf, seq, tm=512):
    m = q_mem.shape[0]
    tiles_per_batch = seq // tm
    return pl.pallas_call(
        _mem_kernel,
        grid=(m // tm,),
        in_specs=[pl.BlockSpec((tm, MEM_WIDTH), lambda i: (i, 0)),
                  pl.BlockSpec((1, MEM_TOKENS, 2 * MEM_WIDTH), lambda i: (i // tiles_per_batch, 0, 0)),
                  pl.BlockSpec(memory_space=pl.ANY)],
        out_specs=pl.BlockSpec((tm, MEM_WIDTH), lambda i: (i, MAIN_WIDTH // MEM_WIDTH)),
        out_shape=jax.ShapeDtypeStruct(buf.shape, buf.dtype),
        input_output_aliases={2: 0},
        compiler_params=_cparams(("parallel",)),
        name="memory_attention",
    )(q_mem, kv, buf)


def kernel(x, mem, mem_norm, ssd_norm, ssd_w_in, ssd_conv_w, ssd_conv_b, ssd_dt_bias, ssd_a_log, ssd_d, ssd_mem_kv, ssd_out_norm, ssd_w_out, sb_norm, sb_w_in, sb_mem_kv, sb_out_norm, sb_w_out, final_norm):
    batch, seq, d = x.shape
    m = batch * seq
    depth = ssd_w_in.shape[0] + sb_w_in.shape[0]
    xf = x.reshape(m, d)
    mem_n = _rmsnorm(mem.reshape(batch * MEM_TOKENS, d), mem_norm, BF16)
    in_norm = lambda layer: (ssd_norm if layer % 2 == 0 else sb_norm)[layer // 2]

    xg, inv = _prep(xf, in_norm(0))
    for layer in range(depth):
        j = layer // 2
        if layer % 2 == 0:
            w_in, w_kv, out_norm, w_out = ssd_w_in, ssd_mem_kv, ssd_out_norm, ssd_w_out
            c0, c1, c2 = SSD_CONV_DIM, SSD_CONV_DIM + SSD_HEADS, SSD_CONV_DIM + SSD_HEADS + MIX_WIDTH
            w_dt = jnp.pad(w_in[j, :, c0:c1], ((0, 0), (0, HEAD_PAD - SSD_HEADS)))[None]
            w_z, lz, z0 = w_in[j:j + 1, :, c1:c2], 0, 0
            w_q, lq, q0 = w_in[j:j + 1, :, c2:], 0, 0
            xbc = _project(xg, w_in, j, 0, c0, F32, inv=inv)
            dt_raw = _project(xg, w_dt, 0, 0, HEAD_PAD, F32, inv=inv)
            buf = _ssd_mixer(xbc, dt_raw, ssd_conv_w[j], ssd_conv_b[j], ssd_dt_bias[j],
                             ssd_a_log[j], ssd_d[j], batch, seq)
        else:
            w_in, w_kv, out_norm, w_out = sb_w_in, sb_mem_kv, sb_out_norm, sb_w_out
            c0, c1 = 3 * MAIN_WIDTH, 3 * MAIN_WIDTH + MIX_WIDTH
            w_z, lz, z0 = w_in, j, c0
            w_q, lq, q0 = w_in, j, c1
            qkv = _project(xg, w_in, j, 0, c0, BF16, inv=inv)
            buf = _sb_mixer(qkv, batch, seq)
        q_mem = _project(xg, w_q, lq, q0, MEM_WIDTH, BF16, inv=inv)
        kv = _project(mem_n, w_kv, j, 0, 2 * MEM_WIDTH, BF16).reshape(batch, MEM_TOKENS, 2 * MEM_WIDTH)
        buf = _mem_attention(q_mem, kv, buf.reshape(m, MIX_WIDTH), seq)
        yg, ssq = _project(xg, w_z, lz, z0, MIX_WIDTH, BF16, inv=inv, gate=(buf, out_norm[j]))
        g_next = in_norm(layer + 1) if layer + 1 < depth else final_norm
        xf, xg, inv = _out_project(yg, ssq, w_out[j].astype(BF16), xf, g_next)

    return _rmsnorm(xf, final_norm, F32).reshape(batch, seq, d)
```

```python
import functools

import jax
import jax.numpy as jnp
from jax import lax
from jax.experimental import pallas as pl
from jax.experimental.pallas import tpu as pltpu

D_MODEL = 2048
MEM_TOKENS = 256
MIX_WIDTH = 2 * D_MODEL
MEM_HEADS = 4
MEM_WIDTH = MIX_WIDTH // 4
MEM_HEAD_DIM = MEM_WIDTH // MEM_HEADS
MAIN_WIDTH = MIX_WIDTH - MEM_WIDTH
SSD_HEAD_DIM = 64
SSD_HEADS = MAIN_WIDTH // SSD_HEAD_DIM
SSD_GROUPS = 8
SSD_HEADS_PER_GROUP = SSD_HEADS // SSD_GROUPS
SSD_STATE = 128
SSD_CONV = 4
SSD_CHUNK = 128
SSD_BC_WIDTH = SSD_GROUPS * SSD_STATE
SSD_CONV_DIM = MAIN_WIDTH + 2 * SSD_BC_WIDTH
SB_HEAD_DIM = 128
SB_HEADS = MAIN_WIDTH // SB_HEAD_DIM
EPS = 1e-6
LOG2E = 1.4426950408889634
SB_QUERY_SCALE = SB_HEAD_DIM ** -0.5 * LOG2E
SB_SUBBLOCKS = 4
SB_EXHAUSTED_LOG2 = -160.0

LANES = 128
SUBLANES = 8
HEAD_PAD = LANES
VMEM_LIMIT = 56 * 1024 * 1024

BF16 = jnp.bfloat16
F32 = jnp.float32


def _cparams(semantics):
    return pltpu.CompilerParams(dimension_semantics=semantics, vmem_limit_bytes=VMEM_LIMIT)


def _rms_kernel(x_ref, g_ref, o_ref):
    x = x_ref[...]
    ms = jnp.mean(x * x, axis=-1, keepdims=True)
    o_ref[...] = (x * lax.rsqrt(ms + EPS) * g_ref[...]).astype(o_ref.dtype)


def _rmsnorm(x, g, out_dtype, tm=512):
    m, d = x.shape
    tm = min(tm, m)
    return pl.pallas_call(
        _rms_kernel,
        grid=(m // tm,),
        in_specs=[pl.BlockSpec((tm, d), lambda i: (i, 0)),
                  pl.BlockSpec((1, d), lambda i: (0, 0))],
        out_specs=pl.BlockSpec((tm, d), lambda i: (i, 0)),
        out_shape=jax.ShapeDtypeStruct((m, d), out_dtype),
        compiler_params=_cparams(("parallel",)),
        name="rmsnorm",
    )(x, g.reshape(1, d))


def _inv_rms(sum_sq, width):
    return lax.rsqrt(sum_sq * (1.0 / width) + EPS)


def _prep_kernel(x_ref, g_ref, xg_ref, inv_ref):
    x = x_ref[...]
    xg_ref[...] = (x * g_ref[...]).astype(BF16)
    inv = _inv_rms(jnp.sum(x * x, axis=-1, keepdims=True), x.shape[-1])
    inv_ref[...] = jnp.broadcast_to(inv, inv_ref.shape)


def _prep(x, g, tm=512):
    m, d = x.shape
    return pl.pallas_call(
        _prep_kernel,
        grid=(m // tm,),
        in_specs=[pl.BlockSpec((tm, d), lambda i: (i, 0)),
                  pl.BlockSpec((1, d), lambda i: (0, 0))],
        out_specs=[pl.BlockSpec((tm, d), lambda i: (i, 0)),
                   pl.BlockSpec((tm, LANES), lambda i: (i, 0))],
        out_shape=[jax.ShapeDtypeStruct((m, d), BF16), jax.ShapeDtypeStruct((m, LANES), F32)],
        compiler_params=_cparams(("parallel",)),
        name="norm_prep",
    )(x, g.reshape(1, d))


def _cast_weight_tile(w_ref, wb_ref):
    @pl.when(pl.program_id(1) == 0)
    def _():
        wb_ref[...] = w_ref[...].astype(BF16)


def _proj_kernel(x_ref, w_ref, o_ref, wb_ref):
    _cast_weight_tile(w_ref, wb_ref)
    o_ref[...] = jnp.dot(x_ref[...], wb_ref[...], preferred_element_type=F32).astype(o_ref.dtype)


def _proj_inv_kernel(x_ref, inv_ref, w_ref, o_ref, wb_ref, *, scaled_tiles, tile_scale):
    _cast_weight_tile(w_ref, wb_ref)
    acc = jnp.dot(x_ref[...], wb_ref[...], preferred_element_type=F32)
    row_scale = inv_ref[:, 0:1]
    if scaled_tiles:
        row_scale = row_scale * jnp.where(pl.program_id(0) < scaled_tiles, tile_scale, 1.0)
    o_ref[...] = (acc * row_scale).astype(o_ref.dtype)


def _gate_kernel(x_ref, inv_ref, w_ref, other_ref, g_ref, yg_ref, ssq_ref, wb_ref):
    _cast_weight_tile(w_ref, wb_ref)
    z = jnp.dot(x_ref[...], wb_ref[...], preferred_element_type=F32) * inv_ref[:, 0:1]
    y = other_ref[...] * _silu(z)
    yg_ref[...] = (y * g_ref[...]).astype(BF16)
    ssq_ref[...] = jnp.broadcast_to(jnp.sum(y * y, axis=-1, keepdims=True), ssq_ref.shape)


def _project(x, w, layer, col0, n, out_dtype, inv=None, gate=None, lead_scale=None, tm=1024, tn=1024):
    m, k = x.shape
    tm = min(tm, m)
    tn = min(tn, n)
    assert col0 % tn == 0 and n % tn == 0 and m % tm == 0
    j0 = col0 // tn
    x_spec = pl.BlockSpec((tm, k), lambda j, i: (i, 0))
    inv_spec = pl.BlockSpec((tm, LANES), lambda j, i: (i, 0))
    w_spec = pl.BlockSpec((None, k, tn), lambda j, i: (layer, 0, j0 + j))
    tile_spec = pl.BlockSpec((tm, tn), lambda j, i: (i, j))
    common = dict(grid=(n // tn, m // tm), scratch_shapes=[pltpu.VMEM((k, tn), BF16)],
                  compiler_params=_cparams(("parallel", "arbitrary")))
    if gate is not None:
        other, g = gate
        return pl.pallas_call(
            _gate_kernel,
            in_specs=[x_spec, inv_spec, w_spec, tile_spec, pl.BlockSpec((1, tn), lambda j, i: (0, j))],
            out_specs=[tile_spec, pl.BlockSpec((None, tm, LANES), lambda j, i: (j, i, 0))],
            out_shape=[jax.ShapeDtypeStruct((m, n), BF16),
                       jax.ShapeDtypeStruct((n // tn, m, LANES), F32)],
            name="gate_projection", **common,
        )(x, inv, w, other, g.reshape(1, n))
    if inv is not None:
        width, c = lead_scale if lead_scale is not None else (0, 1.0)
        assert width % tn == 0
        body = functools.partial(_proj_inv_kernel, scaled_tiles=width // tn, tile_scale=c)
        return pl.pallas_call(
            body, in_specs=[x_spec, inv_spec, w_spec], out_specs=tile_spec,
            out_shape=jax.ShapeDtypeStruct((m, n), out_dtype), name="projection", **common,
        )(x, inv, w)
    return pl.pallas_call(
        _proj_kernel, in_specs=[x_spec, w_spec], out_specs=tile_spec,
        out_shape=jax.ShapeDtypeStruct((m, n), out_dtype), name="projection_plain", **common,
    )(x, w)


def _out_kernel(yg_ref, ssq_ref, w_ref, res_ref, gn_ref, x_ref, xg_ref, invn_ref, invy_ref, acc_ref):
    j = pl.program_id(1)

    @pl.when(j == 0)
    def _():
        total = ssq_ref[0, :, 0:1]
        for p in range(1, ssq_ref.shape[0]):
            total = total + ssq_ref[p, :, 0:1]
        invy_ref[...] = _inv_rms(total, MIX_WIDTH)
        acc_ref[...] = jnp.zeros_like(acc_ref)

    x_new = res_ref[...] + jnp.dot(yg_ref[...], w_ref[...], preferred_element_type=F32) * invy_ref[...]
    x_ref[...] = x_new
    xg_ref[...] = (x_new * gn_ref[...]).astype(BF16)
    acc_ref[...] += jnp.sum(x_new * x_new, axis=-1, keepdims=True)

    @pl.when(j == pl.num_programs(1) - 1)
    def _():
        invn_ref[...] = jnp.broadcast_to(_inv_rms(acc_ref[...], D_MODEL), invn_ref.shape)


def _out_project(yg, ssq, w, res, g_next, tm=1024, tn=512):
    m, k = yg.shape
    n = w.shape[1]
    parts = ssq.shape[0]
    return pl.pallas_call(
        _out_kernel,
        grid=(m // tm, n // tn),
        in_specs=[pl.BlockSpec((tm, k), lambda i, j: (i, 0)),
                  pl.BlockSpec((parts, tm, LANES), lambda i, j: (0, i, 0)),
                  pl.BlockSpec((k, tn), lambda i, j: (0, j)),
                  pl.BlockSpec((tm, tn), lambda i, j: (i, j)),
                  pl.BlockSpec((1, tn), lambda i, j: (0, j))],
        out_specs=[pl.BlockSpec((tm, tn), lambda i, j: (i, j)),
                   pl.BlockSpec((tm, tn), lambda i, j: (i, j)),
                   pl.BlockSpec((tm, LANES), lambda i, j: (i, 0))],
        out_shape=[jax.ShapeDtypeStruct((m, n), F32), jax.ShapeDtypeStruct((m, n), BF16),
                   jax.ShapeDtypeStruct((m, LANES), F32)],
        scratch_shapes=[pltpu.VMEM((tm, 1), F32), pltpu.VMEM((tm, 1), F32)],
        compiler_params=_cparams(("parallel", "arbitrary")),
        name="out_projection",
    )(yg, ssq, w, res, g_next.reshape(1, n))


def _split3(x):
    p0 = x.astype(BF16)
    r1 = x - p0.astype(F32)
    p1 = r1.astype(BF16)
    p2 = (r1 - p1.astype(F32)).astype(BF16)
    return p0, p1, p2


def _dot_f32_by_01(x, ones_rhs):
    p0, p1, p2 = _split3(x)
    d = functools.partial(jnp.dot, preferred_element_type=F32)
    return d(p0, ones_rhs) + (d(p1, ones_rhs) + d(p2, ones_rhs))


def _dot_01_by_f32(ones_lhs, x):
    p0, p1, p2 = _split3(x)
    d = functools.partial(jnp.dot, preferred_element_type=F32)
    return d(ones_lhs, p0) + (d(ones_lhs, p1) + d(ones_lhs, p2))


def _softplus(x):
    return jnp.maximum(x, 0.0) + jnp.log(1.0 + jnp.exp(-jnp.abs(x)))


def _silu(x):
    return x / (1.0 + jnp.exp(-x))


def _ssd_kernel(xbc_ref, dt_ref, cw_ref, cb_ref, dtb_ref, alog_ref, expand_ref, dskip_ref,
                y_ref, ext_ref, carry_ref, xc_ref, ex_ref, st_ref):
    q = SSD_CHUNK
    hp = SSD_HEAD_DIM
    gw = SSD_HEADS_PER_GROUP * hp

    @pl.when(pl.program_id(1) == 0)
    def _():
        carry_ref[...] = jnp.zeros_like(carry_ref)
        st_ref[...] = jnp.zeros_like(st_ref)

    u = xbc_ref[0]
    shifted = None
    for tap in range(SSD_CONV - 1):
        part = cw_ref[tap:tap + 1, :] * u
        if shifted is not None:
            part = part + shifted
        ext_ref[...] = pltpu.roll(part, 1, axis=0)
        last_row = ext_ref[0:1, :]
        ext_ref[0:1, :] = carry_ref[tap:tap + 1, :]
        carry_ref[tap:tap + 1, :] = last_row
        shifted = ext_ref[...]
    conv = cb_ref[...] + cw_ref[SSD_CONV - 1:SSD_CONV, :] * u + shifted
    xc_ref[...] = _silu(conv)

    dt = _softplus(dt_ref[0] + dtb_ref[...])
    d_a = dt * (-jnp.exp(alog_ref[...]))
    row = lax.broadcasted_iota(jnp.int32, (q, q), 0)
    col = lax.broadcasted_iota(jnp.int32, (q, q), 1)
    causal = row >= col
    cum = _dot_01_by_f32(causal.astype(BF16), d_a)
    cum_t = cum.T
    cum_end = cum[q - 1:q, :]
    to_end = jnp.exp(cum_end - cum) * dt
    ecum = jnp.exp(cum)

    expand = expand_ref[...]
    ex_ref[0:2 * q, :] = jnp.dot(jnp.concatenate([dt, to_end], axis=0).astype(BF16), expand,
                                 preferred_element_type=F32)
    ecum_hi = ecum.astype(BF16)
    ecum_lo = (ecum - ecum_hi.astype(F32)).astype(BF16)
    ex_ref[2 * q:3 * q, :] = (jnp.dot(ecum_hi, expand, preferred_element_type=F32)
                              + jnp.dot(ecum_lo, expand, preferred_element_type=F32))

    lane = lax.broadcasted_iota(jnp.int32, (q, LANES), 1)
    low_half = lane < hp

    for g in range(SSD_GROUPS):
        bg = xc_ref[:, MAIN_WIDTH + g * SSD_STATE:MAIN_WIDTH + (g + 1) * SSD_STATE]
        cg = xc_ref[:, MAIN_WIDTH + SSD_BC_WIDTH + g * SSD_STATE:
                    MAIN_WIDTH + SSD_BC_WIDTH + (g + 1) * SSD_STATE]
        cg16 = cg.astype(BF16)
        cb = lax.dot_general(cg16, bg.astype(BF16), (((1,), (1,)), ((), ())),
                             preferred_element_type=F32)
        gs = slice(g * gw, (g + 1) * gw)
        y_off = jnp.dot(cg16, st_ref[:, gs].astype(BF16), preferred_element_type=F32)

        for pair in range(SSD_HEADS_PER_GROUP // 2):
            ws = []
            for r in (2 * pair, 2 * pair + 1):
                h = g * SSD_HEADS_PER_GROUP + r
                seg = cum[:, h:h + 1] - cum_t[h:h + 1, :]
                ws.append(cb * jnp.exp(jnp.where(causal, seg, -jnp.inf)))
            w2 = jnp.concatenate(ws, axis=1).astype(BF16)
            ps = slice(g * gw + pair * LANES, g * gw + (pair + 1) * LANES)
            xs = xc_ref[:, ps]
            xdt = xs * ex_ref[0:q, ps]
            rhs = jnp.concatenate([jnp.where(low_half, xdt, 0.0),
                                   jnp.where(low_half, 0.0, xdt)], axis=0).astype(BF16)
            y_diag = jnp.dot(w2, rhs, preferred_element_type=F32)
            yo = y_off[:, pair * LANES:(pair + 1) * LANES]
            y_ref[0, :, ps] = y_diag + ex_ref[2 * q:3 * q, ps] * yo + xs * dskip_ref[:, ps]

        xte = (xc_ref[:, gs] * ex_ref[q:2 * q, gs]).astype(BF16)
        upd = jnp.dot(bg.T.astype(BF16), xte, preferred_element_type=F32)
        st_ref[:, gs] = st_ref[:, gs] * ex_ref[3 * q - 1:3 * q, gs] + upd


def _ssd_mixer(xbc, dt_raw, conv_w, conv_b, dt_bias, a_log, d_skip, batch, seq):
    q = SSD_CHUNK
    nchunks = seq // q
    pad = HEAD_PAD - SSD_HEADS
    dtb = jnp.pad(dt_bias, (0, pad)).reshape(1, HEAD_PAD)
    alog = jnp.pad(a_log, (0, pad)).reshape(1, HEAD_PAD)
    head_of_lane = jnp.arange(MAIN_WIDTH) // SSD_HEAD_DIM
    expand = (jnp.arange(HEAD_PAD)[:, None] == head_of_lane[None, :]).astype(BF16)
    dskip = jnp.repeat(d_skip, SSD_HEAD_DIM).reshape(1, MAIN_WIDTH)
    const = lambda shape: pl.BlockSpec(shape, lambda b, c: (0,) * len(shape))
    return pl.pallas_call(
        _ssd_kernel,
        grid=(batch, nchunks),
        in_specs=[pl.BlockSpec((1, q, SSD_CONV_DIM), lambda b, c: (b, c, 0)),
                  pl.BlockSpec((1, q, HEAD_PAD), lambda b, c: (b, c, 0)),
                  const((SSD_CONV, SSD_CONV_DIM)),
                  const((1, SSD_CONV_DIM)),
                  const((1, HEAD_PAD)),
                  const((1, HEAD_PAD)),
                  const((HEAD_PAD, MAIN_WIDTH)),
                  const((1, MAIN_WIDTH))],
        out_specs=pl.BlockSpec((1, q, MAIN_WIDTH), lambda b, c: (b, c, 0)),
        out_shape=jax.ShapeDtypeStruct((batch, seq, MIX_WIDTH), F32),
        scratch_shapes=[pltpu.VMEM((q, SSD_CONV_DIM), F32),
                        pltpu.VMEM((SUBLANES, SSD_CONV_DIM), F32),
                        pltpu.VMEM((q, SSD_CONV_DIM), F32),
                        pltpu.VMEM((3 * q, MAIN_WIDTH), F32),
                        pltpu.VMEM((SSD_STATE, MAIN_WIDTH), F32)],
        compiler_params=_cparams(("parallel", "arbitrary")),
        name="ssd_mixer",
    )(xbc.reshape(batch, seq, SSD_CONV_DIM), dt_raw.reshape(batch, seq, HEAD_PAD),
      conv_w, conv_b.reshape(1, SSD_CONV_DIM), dtb, alog, expand, dskip)


def _sb_kernel(q_ref, k_ref, v_ref, o_ref, *, tile):
    t = tile
    row = lax.broadcasted_iota(jnp.int32, (t, t), 0)
    col = lax.broadcasted_iota(jnp.int32, (t, t), 1)
    strict = row > col
    later = strict.astype(BF16)

    def scores(qb, j):
        kj = k_ref[0, pl.ds(pl.multiple_of(j * t, t), t), :]
        z2 = lax.dot_general(qb, kj, (((1,), (1,)), ((), ())), preferred_element_type=F32)
        neg_abs = lax.bitcast_convert_type(
            lax.bitcast_convert_type(z2, jnp.uint32) | jnp.uint32(0x80000000), F32)
        log_sig = jnp.minimum(z2, 0.0) - jnp.log2(1.0 + jnp.exp2(neg_abs))
        return log_sig, log_sig - z2

    def suffix(log_keep, diagonal):
        if diagonal:
            log_keep = jnp.where(strict, log_keep, 0.0)
        after = jnp.dot(log_keep.astype(BF16), later, preferred_element_type=F32)
        return after, jnp.sum(log_keep, axis=1, keepdims=True)

    def weigh(j, log_sig, after, acc, diagonal):
        vj = v_ref[0, pl.ds(pl.multiple_of(j * t, t), t), :]
        a = jnp.exp2(log_sig + after + acc)
        if diagonal:
            a = jnp.where(strict, a, 0.0)
        return jnp.dot(a.astype(BF16), vj, preferred_element_type=F32)

    def two_blocks(group, diagonal):
        scored = [(scores(qb, j_a), scores(qb, j_b)) for qb, j_a, j_b, _, _, _ in group]
        summed = [(suffix(sa[1], diagonal), suffix(sb[1], False)) for sa, sb in scored]
        out = []
        for (qb, j_a, j_b, use_b, acc, o), (sa, sb), ((aft_a, tot_a), (aft_b, tot_b)) in zip(group, scored, summed):
            acc_a = acc + tot_a
            o_a = o + weigh(j_a, sa[0], aft_a, acc, diagonal)
            o_b = o_a + weigh(j_b, sb[0], aft_b, acc_a, False)
            out.append((jnp.where(use_b, acc_a + tot_b, acc_a), jnp.where(use_b, o_b, o_a)))
        return out

    group = []
    for r in range(SB_SUBBLOCKS):
        i = pl.program_id(2) * SB_SUBBLOCKS + r
        qb = q_ref[0, r * t:(r + 1) * t, :]
        group.append((qb, i, jnp.maximum(i - 1, 0), i >= 1,
                      jnp.zeros((t, 1), F32), jnp.zeros((t, SB_HEAD_DIM), F32)))
    started = [(pl.program_id(2) * SB_SUBBLOCKS + r, g[0], acc, o)
               for r, (g, (acc, o)) in enumerate(zip(group, two_blocks(group, True)))]

    for r, (i, qb, acc, o) in enumerate(started):
        def unfinished(carry, i=i):
            step, acc, _ = carry
            return jnp.logical_and(step < i // 2, jnp.max(acc) > SB_EXHAUSTED_LOG2)

        def body(carry, i=i, qb=qb):
            step, acc, o = carry
            j_a = i - 2 - 2 * step
            (acc, o), = two_blocks([(qb, j_a, jnp.maximum(j_a - 1, 0), j_a >= 1, acc, o)], False)
            return step + 1, acc, o

        _, _, o = lax.while_loop(unfinished, body, (jnp.int32(0), acc, o))
        o_ref[0, r * t:(r + 1) * t, :] = o


def _sb_mixer(qkv, batch, seq, tile=256):
    rows = SB_SUBBLOCKS * tile
    assert seq % rows == 0
    nq = seq // rows
    qkv3 = qkv.reshape(batch, seq, 3 * MAIN_WIDTH)
    return pl.pallas_call(
        functools.partial(_sb_kernel, tile=tile),
        grid=(batch, SB_HEADS, nq),
        in_specs=[pl.BlockSpec((1, rows, SB_HEAD_DIM), lambda b, h, i: (b, i, h)),
                  pl.BlockSpec((1, seq, SB_HEAD_DIM), lambda b, h, i: (b, 0, SB_HEADS + h)),
                  pl.BlockSpec((1, seq, SB_HEAD_DIM), lambda b, h, i: (b, 0, 2 * SB_HEADS + h))],
        out_specs=pl.BlockSpec((1, rows, SB_HEAD_DIM), lambda b, h, i: (b, i, h)),
        out_shape=jax.ShapeDtypeStruct((batch, seq, MIX_WIDTH), F32),
        compiler_params=_cparams(("parallel", "parallel", "arbitrary")),
        name="stick_breaking",
    )(qkv3, qkv3, qkv3)


def _mem_kernel(qm_ref, kv_ref, buf_ref, o_ref):
    del buf_ref
    scale = MEM_HEAD_DIM ** -0.5
    for h in range(MEM_HEADS):
        hs = slice(h * MEM_HEAD_DIM, (h + 1) * MEM_HEAD_DIM)
        kh = kv_ref[0, :, hs]
        vh = kv_ref[0, :, MEM_WIDTH + h * MEM_HEAD_DIM:MEM_WIDTH + (h + 1) * MEM_HEAD_DIM]
        s = lax.dot_general(qm_ref[:, hs], kh, (((1,), (1,)), ((), ())),
                            preferred_element_type=F32) * scale
        e = jnp.exp(s - jnp.max(s, axis=-1, keepdims=True))
        p = e * (1.0 / jnp.sum(e, axis=-1, keepdims=True))
        o_ref[:, hs] = jnp.dot(p.astype(BF16), vh, preferred_element_type=F32)


def _mem_attention(q_mem, kv, buf, seq, tm=512):
    m = q_mem.shape[0]
    tiles_per_batch = seq // tm
    return pl.pallas_call(
        _mem_kernel,
        grid=(m // tm,),
        in_specs=[pl.BlockSpec((tm, MEM_WIDTH), lambda i: (i, 0)),
                  pl.BlockSpec((1, MEM_TOKENS, 2 * MEM_WIDTH), lambda i: (i // tiles_per_batch, 0, 0)),
                  pl.BlockSpec(memory_space=pl.ANY)],
        out_specs=pl.BlockSpec((tm, MEM_WIDTH), lambda i: (i, MAIN_WIDTH // MEM_WIDTH)),
        out_shape=jax.ShapeDtypeStruct(buf.shape, buf.dtype),
        input_output_aliases={2: 0},
        compiler_params=_cparams(("parallel",)),
        name="memory_attention",
    )(q_mem, kv, buf)


def kernel(x, mem, mem_norm, ssd_norm, ssd_w_in, ssd_conv_w, ssd_conv_b, ssd_dt_bias, ssd_a_log, ssd_d, ssd_mem_kv, ssd_out_norm, ssd_w_out, sb_norm, sb_w_in, sb_mem_kv, sb_out_norm, sb_w_out, final_norm):
    batch, seq, d = x.shape
    m = batch * seq
    depth = ssd_w_in.shape[0] + sb_w_in.shape[0]
    xf = x.reshape(m, d)
    mem_n = _rmsnorm(mem.reshape(batch * MEM_TOKENS, d), mem_norm, BF16)
    in_norm = lambda layer: (ssd_norm if layer % 2 == 0 else sb_norm)[layer // 2]

    xg, inv = _prep(xf, in_norm(0))
    for layer in range(depth):
        j = layer // 2
        if layer % 2 == 0:
            w_in, w_kv, out_norm, w_out = ssd_w_in, ssd_mem_kv, ssd_out_norm, ssd_w_out
            c0, c1, c2 = SSD_CONV_DIM, SSD_CONV_DIM + SSD_HEADS, SSD_CONV_DIM + SSD_HEADS + MIX_WIDTH
            w_dt = jnp.pad(w_in[j, :, c0:c1], ((0, 0), (0, HEAD_PAD - SSD_HEADS)))[None]
            w_z, lz, z0 = w_in[j:j + 1, :, c1:c2], 0, 0
            w_q, lq, q0 = w_in[j:j + 1, :, c2:], 0, 0
            xbc = _project(xg, w_in, j, 0, c0, F32, inv=inv)
            dt_raw = _project(xg, w_dt, 0, 0, HEAD_PAD, F32, inv=inv)
            buf = _ssd_mixer(xbc, dt_raw, ssd_conv_w[j], ssd_conv_b[j], ssd_dt_bias[j],
                             ssd_a_log[j], ssd_d[j], batch, seq)
        else:
            w_in, w_kv, out_norm, w_out = sb_w_in, sb_mem_kv, sb_out_norm, sb_w_out
            c0, c1 = 3 * MAIN_WIDTH, 3 * MAIN_WIDTH + MIX_WIDTH
            w_z, lz, z0 = w_in, j, c0
            w_q, lq, q0 = w_in, j, c1
            qkv = _project(xg, w_in, j, 0, c0, BF16, inv=inv, lead_scale=(MAIN_WIDTH, SB_QUERY_SCALE))
            buf = _sb_mixer(qkv, batch, seq)
        q_mem = _project(xg, w_q, lq, q0, MEM_WIDTH, BF16, inv=inv)
        kv = _project(mem_n, w_kv, j, 0, 2 * MEM_WIDTH, BF16).reshape(batch, MEM_TOKENS, 2 * MEM_WIDTH)
        buf = _mem_attention(q_mem, kv, buf.reshape(m, MIX_WIDTH), seq)
        yg, ssq = _project(xg, w_z, lz, z0, MIX_WIDTH, BF16, inv=inv, gate=(buf, out_norm[j]))
        g_next = in_norm(layer + 1) if layer + 1 < depth else final_norm
        xf, xg, inv = _out_project(yg, ssq, w_out[j].astype(BF16), xf, g_next)

    return _rmsnorm(xf, final_norm, F32).reshape(batch, seq, d)
```

```python
import functools

import jax
import jax.numpy as jnp
from jax import lax
from jax.experimental import pallas as pl
from jax.experimental.pallas import tpu as pltpu

D_MODEL = 2048
MEM_TOKENS = 256
MIX_WIDTH = 2 * D_MODEL
MEM_HEADS = 4
MEM_WIDTH = MIX_WIDTH // 4
MEM_HEAD_DIM = MEM_WIDTH // MEM_HEADS
MAIN_WIDTH = MIX_WIDTH - MEM_WIDTH
SSD_HEAD_DIM = 64
SSD_HEADS = MAIN_WIDTH // SSD_HEAD_DIM
SSD_GROUPS = 8
SSD_HEADS_PER_GROUP = SSD_HEADS // SSD_GROUPS
SSD_STATE = 128
SSD_CONV = 4
SSD_CHUNK = 128
SSD_BC_WIDTH = SSD_GROUPS * SSD_STATE
SSD_CONV_DIM = MAIN_WIDTH + 2 * SSD_BC_WIDTH
SB_HEAD_DIM = 128
SB_HEADS = MAIN_WIDTH // SB_HEAD_DIM
EPS = 1e-6
LOG2E = 1.4426950408889634
SB_QUERY_SCALE = SB_HEAD_DIM ** -0.5 * LOG2E
SB_SUBBLOCKS = 4
SB_EXHAUSTED_LOG2 = -160.0

LANES = 128
SUBLANES = 8
HEAD_PAD = LANES
VMEM_LIMIT = 56 * 1024 * 1024

BF16 = jnp.bfloat16
F32 = jnp.float32


def _cparams(semantics):
    return pltpu.CompilerParams(dimension_semantics=semantics, vmem_limit_bytes=VMEM_LIMIT)


def _rms_kernel(x_ref, g_ref, o_ref):
    x = x_ref[...]
    ms = jnp.mean(x * x, axis=-1, keepdims=True)
    o_ref[...] = (x * lax.rsqrt(ms + EPS) * g_ref[...]).astype(o_ref.dtype)


def _rmsnorm(x, g, out_dtype, tm=512):
    m, d = x.shape
    tm = min(tm, m)
    return pl.pallas_call(
        _rms_kernel,
        grid=(m // tm,),
        in_specs=[pl.BlockSpec((tm, d), lambda i: (i, 0)),
                  pl.BlockSpec((1, d), lambda i: (0, 0))],
        out_specs=pl.BlockSpec((tm, d), lambda i: (i, 0)),
        out_shape=jax.ShapeDtypeStruct((m, d), out_dtype),
        compiler_params=_cparams(("parallel",)),
        name="rmsnorm",
    )(x, g.reshape(1, d))


def _inv_rms(sum_sq, width):
    return lax.rsqrt(sum_sq * (1.0 / width) + EPS)


def _prep_kernel(x_ref, g_ref, xg_ref, inv_ref):
    x = x_ref[...]
    xg_ref[...] = (x * g_ref[...]).astype(BF16)
    inv = _inv_rms(jnp.sum(x * x, axis=-1, keepdims=True), x.shape[-1])
    inv_ref[...] = jnp.broadcast_to(inv, inv_ref.shape)


def _prep(x, g, tm=512):
    m, d = x.shape
    return pl.pallas_call(
        _prep_kernel,
        grid=(m // tm,),
        in_specs=[pl.BlockSpec((tm, d), lambda i: (i, 0)),
                  pl.BlockSpec((1, d), lambda i: (0, 0))],
        out_specs=[pl.BlockSpec((tm, d), lambda i: (i, 0)),
                   pl.BlockSpec((tm, LANES), lambda i: (i, 0))],
        out_shape=[jax.ShapeDtypeStruct((m, d), BF16), jax.ShapeDtypeStruct((m, LANES), F32)],
        compiler_params=_cparams(("parallel",)),
        name="norm_prep",
    )(x, g.reshape(1, d))


def _cast_weight_tile(w_ref, wb_ref):
    @pl.when(pl.program_id(1) == 0)
    def _():
        wb_ref[...] = w_ref[...].astype(BF16)


def _proj_kernel(x_ref, w_ref, o_ref, wb_ref):
    _cast_weight_tile(w_ref, wb_ref)
    o_ref[...] = jnp.dot(x_ref[...], wb_ref[...], preferred_element_type=F32).astype(o_ref.dtype)


def _proj_inv_kernel(x_ref, inv_ref, w_ref, o_ref, wb_ref, *, scaled_tiles, tile_scale):
    _cast_weight_tile(w_ref, wb_ref)
    acc = jnp.dot(x_ref[...], wb_ref[...], preferred_element_type=F32)
    row_scale = inv_ref[:, 0:1]
    if scaled_tiles:
        row_scale = row_scale * jnp.where(pl.program_id(0) < scaled_tiles, tile_scale, 1.0)
    o_ref[...] = (acc * row_scale).astype(o_ref.dtype)


def _gate_kernel(x_ref, inv_ref, w_ref, main_ref, mem_ref, g_ref, yg_ref, ssq_ref, wb_ref, *, main_tiles):
    _cast_weight_tile(w_ref, wb_ref)
    z = jnp.dot(x_ref[...], wb_ref[...], preferred_element_type=F32) * inv_ref[:, 0:1]
    other = jnp.where(pl.program_id(0) < main_tiles, main_ref[...], mem_ref[...])
    y = other * _silu(z)
    yg_ref[...] = (y * g_ref[...]).astype(BF16)
    ssq_ref[...] = jnp.broadcast_to(jnp.sum(y * y, axis=-1, keepdims=True), ssq_ref.shape)


def _project(x, w, layer, col0, n, out_dtype, inv=None, gate=None, lead_scale=None, tm=1024, tn=1024):
    m, k = x.shape
    tm = min(tm, m)
    tn = min(tn, n)
    assert col0 % tn == 0 and n % tn == 0 and m % tm == 0
    j0 = col0 // tn
    x_spec = pl.BlockSpec((tm, k), lambda j, i: (i, 0))
    inv_spec = pl.BlockSpec((tm, LANES), lambda j, i: (i, 0))
    w_spec = pl.BlockSpec((None, k, tn), lambda j, i: (layer, 0, j0 + j))
    tile_spec = pl.BlockSpec((tm, tn), lambda j, i: (i, j))
    common = dict(grid=(n // tn, m // tm), scratch_shapes=[pltpu.VMEM((k, tn), BF16)],
                  compiler_params=_cparams(("parallel", "arbitrary")))
    if gate is not None:
        main, mem, g = gate
        assert main.shape[1] % tn == 0 and mem.shape[1] == tn and main.shape[1] + tn == n
        main_tiles = main.shape[1] // tn
        main_spec = pl.BlockSpec((tm, tn), lambda j, i: (jnp.where(j < main_tiles, i, 0),
                                                         jnp.minimum(j, main_tiles - 1)))
        mem_spec = pl.BlockSpec((tm, tn), lambda j, i: (jnp.where(j < main_tiles, 0, i), 0))
        return pl.pallas_call(
            functools.partial(_gate_kernel, main_tiles=main_tiles),
            in_specs=[x_spec, inv_spec, w_spec, main_spec, mem_spec,
                      pl.BlockSpec((1, tn), lambda j, i: (0, j))],
            out_specs=[tile_spec, pl.BlockSpec((None, tm, LANES), lambda j, i: (j, i, 0))],
            out_shape=[jax.ShapeDtypeStruct((m, n), BF16),
                       jax.ShapeDtypeStruct((n // tn, m, LANES), F32)],
            name="gate_projection", **common,
        )(x, inv, w, main, mem, g.reshape(1, n))
    if inv is not None:
        width, c = lead_scale if lead_scale is not None else (0, 1.0)
        assert width % tn == 0
        body = functools.partial(_proj_inv_kernel, scaled_tiles=width // tn, tile_scale=c)
        return pl.pallas_call(
            body, in_specs=[x_spec, inv_spec, w_spec], out_specs=tile_spec,
            out_shape=jax.ShapeDtypeStruct((m, n), out_dtype), name="projection", **common,
        )(x, inv, w)
    return pl.pallas_call(
        _proj_kernel, in_specs=[x_spec, w_spec], out_specs=tile_spec,
        out_shape=jax.ShapeDtypeStruct((m, n), out_dtype), name="projection_plain", **common,
    )(x, w)


def _out_kernel(yg_ref, ssq_ref, w_ref, res_ref, gn_ref, x_ref, xg_ref, invn_ref, invy_ref, acc_ref):
    j = pl.program_id(1)

    @pl.when(j == 0)
    def _():
        total = ssq_ref[0, :, 0:1]
        for p in range(1, ssq_ref.shape[0]):
            total = total + ssq_ref[p, :, 0:1]
        invy_ref[...] = _inv_rms(total, MIX_WIDTH)
        acc_ref[...] = jnp.zeros_like(acc_ref)

    x_new = res_ref[...] + jnp.dot(yg_ref[...], w_ref[...], preferred_element_type=F32) * invy_ref[...]
    x_ref[...] = x_new
    xg_ref[...] = (x_new * gn_ref[...]).astype(BF16)
    acc_ref[...] += jnp.sum(x_new * x_new, axis=-1, keepdims=True)

    @pl.when(j == pl.num_programs(1) - 1)
    def _():
        invn_ref[...] = jnp.broadcast_to(_inv_rms(acc_ref[...], D_MODEL), invn_ref.shape)


def _out_project(yg, ssq, w, res, g_next, tm=1024, tn=512):
    m, k = yg.shape
    n = w.shape[1]
    parts = ssq.shape[0]
    return pl.pallas_call(
        _out_kernel,
        grid=(m // tm, n // tn),
        in_specs=[pl.BlockSpec((tm, k), lambda i, j: (i, 0)),
                  pl.BlockSpec((parts, tm, LANES), lambda i, j: (0, i, 0)),
                  pl.BlockSpec((k, tn), lambda i, j: (0, j)),
                  pl.BlockSpec((tm, tn), lambda i, j: (i, j)),
                  pl.BlockSpec((1, tn), lambda i, j: (0, j))],
        out_specs=[pl.BlockSpec((tm, tn), lambda i, j: (i, j)),
                   pl.BlockSpec((tm, tn), lambda i, j: (i, j)),
                   pl.BlockSpec((tm, LANES), lambda i, j: (i, 0))],
        out_shape=[jax.ShapeDtypeStruct((m, n), F32), jax.ShapeDtypeStruct((m, n), BF16),
                   jax.ShapeDtypeStruct((m, LANES), F32)],
        scratch_shapes=[pltpu.VMEM((tm, 1), F32), pltpu.VMEM((tm, 1), F32)],
        compiler_params=_cparams(("parallel", "arbitrary")),
        name="out_projection",
    )(yg, ssq, w, res, g_next.reshape(1, n))


def _split3(x):
    p0 = x.astype(BF16)
    r1 = x - p0.astype(F32)
    p1 = r1.astype(BF16)
    p2 = (r1 - p1.astype(F32)).astype(BF16)
    return p0, p1, p2


def _dot_f32_by_01(x, ones_rhs):
    p0, p1, p2 = _split3(x)
    d = functools.partial(jnp.dot, preferred_element_type=F32)
    return d(p0, ones_rhs) + (d(p1, ones_rhs) + d(p2, ones_rhs))


def _dot_01_by_f32(ones_lhs, x):
    p0, p1, p2 = _split3(x)
    d = functools.partial(jnp.dot, preferred_element_type=F32)
    return d(ones_lhs, p0) + (d(ones_lhs, p1) + d(ones_lhs, p2))


def _softplus(x):
    return jnp.maximum(x, 0.0) + jnp.log(1.0 + jnp.exp(-jnp.abs(x)))


def _silu(x):
    return x / (1.0 + jnp.exp(-x))


def _ssd_kernel(xbc_ref, dt_ref, cw_ref, cb_ref, dtb_ref, alog_ref, expand_ref, dskip_ref,
                y_ref, ext_ref, carry_ref, xc_ref, ex_ref, st_ref):
    q = SSD_CHUNK
    hp = SSD_HEAD_DIM
    gw = SSD_HEADS_PER_GROUP * hp

    @pl.when(pl.program_id(1) == 0)
    def _():
        carry_ref[...] = jnp.zeros_like(carry_ref)
        st_ref[...] = jnp.zeros_like(st_ref)

    u = xbc_ref[0]
    shifted = None
    for tap in range(SSD_CONV - 1):
        part = cw_ref[tap:tap + 1, :] * u
        if shifted is not None:
            part = part + shifted
        ext_ref[...] = pltpu.roll(part, 1, axis=0)
        last_row = ext_ref[0:1, :]
        ext_ref[0:1, :] = carry_ref[tap:tap + 1, :]
        carry_ref[tap:tap + 1, :] = last_row
        shifted = ext_ref[...]
    conv = cb_ref[...] + cw_ref[SSD_CONV - 1:SSD_CONV, :] * u + shifted
    xc_ref[...] = _silu(conv)

    dt = _softplus(dt_ref[0] + dtb_ref[...])
    d_a = dt * (-jnp.exp(alog_ref[...]))
    row = lax.broadcasted_iota(jnp.int32, (q, q), 0)
    col = lax.broadcasted_iota(jnp.int32, (q, q), 1)
    causal = row >= col
    cum = _dot_01_by_f32(causal.astype(BF16), d_a)
    cum_t = cum.T
    cum_end = cum[q - 1:q, :]
    to_end = jnp.exp(cum_end - cum) * dt
    ecum = jnp.exp(cum)

    expand = expand_ref[...]
    ex_ref[0:2 * q, :] = jnp.dot(jnp.concatenate([dt, to_end], axis=0).astype(BF16), expand,
                                 preferred_element_type=F32)
    ecum_hi = ecum.astype(BF16)
    ecum_lo = (ecum - ecum_hi.astype(F32)).astype(BF16)
    ex_ref[2 * q:3 * q, :] = (jnp.dot(ecum_hi, expand, preferred_element_type=F32)
                              + jnp.dot(ecum_lo, expand, preferred_element_type=F32))

    lane = lax.broadcasted_iota(jnp.int32, (q, LANES), 1)
    low_half = lane < hp

    for g in range(SSD_GROUPS):
        bg = xc_ref[:, MAIN_WIDTH + g * SSD_STATE:MAIN_WIDTH + (g + 1) * SSD_STATE]
        cg = xc_ref[:, MAIN_WIDTH + SSD_BC_WIDTH + g * SSD_STATE:
                    MAIN_WIDTH + SSD_BC_WIDTH + (g + 1) * SSD_STATE]
        cg16 = cg.astype(BF16)
        cb = lax.dot_general(cg16, bg.astype(BF16), (((1,), (1,)), ((), ())),
                             preferred_element_type=F32)
        gs = slice(g * gw, (g + 1) * gw)
        y_off = jnp.dot(cg16, st_ref[:, gs].astype(BF16), preferred_element_type=F32)

        for pair in range(SSD_HEADS_PER_GROUP // 2):
            ws = []
            for r in (2 * pair, 2 * pair + 1):
                h = g * SSD_HEADS_PER_GROUP + r
                seg = cum[:, h:h + 1] - cum_t[h:h + 1, :]
                ws.append(cb * jnp.exp(jnp.where(causal, seg, -jnp.inf)))
            w2 = jnp.concatenate(ws, axis=1).astype(BF16)
            ps = slice(g * gw + pair * LANES, g * gw + (pair + 1) * LANES)
            xs = xc_ref[:, ps]
            xdt = xs * ex_ref[0:q, ps]
            rhs = jnp.concatenate([jnp.where(low_half, xdt, 0.0),
                                   jnp.where(low_half, 0.0, xdt)], axis=0).astype(BF16)
            y_diag = jnp.dot(w2, rhs, preferred_element_type=F32)
            yo = y_off[:, pair * LANES:(pair + 1) * LANES]
            y_ref[0, :, ps] = y_diag + ex_ref[2 * q:3 * q, ps] * yo + xs * dskip_ref[:, ps]

        xte = (xc_ref[:, gs] * ex_ref[q:2 * q, gs]).astype(BF16)
        upd = jnp.dot(bg.T.astype(BF16), xte, preferred_element_type=F32)
        st_ref[:, gs] = st_ref[:, gs] * ex_ref[3 * q - 1:3 * q, gs] + upd


def _ssd_mixer(xbc, dt_raw, conv_w, conv_b, dt_bias, a_log, d_skip, batch, seq):
    q = SSD_CHUNK
    nchunks = seq // q
    pad = HEAD_PAD - SSD_HEADS
    dtb = jnp.pad(dt_bias, (0, pad)).reshape(1, HEAD_PAD)
    alog = jnp.pad(a_log, (0, pad)).reshape(1, HEAD_PAD)
    head_of_lane = jnp.arange(MAIN_WIDTH) // SSD_HEAD_DIM
    expand = (jnp.arange(HEAD_PAD)[:, None] == head_of_lane[None, :]).astype(BF16)
    dskip = jnp.repeat(d_skip, SSD_HEAD_DIM).reshape(1, MAIN_WIDTH)
    const = lambda shape: pl.BlockSpec(shape, lambda b, c: (0,) * len(shape))
    return pl.pallas_call(
        _ssd_kernel,
        grid=(batch, nchunks),
        in_specs=[pl.BlockSpec((1, q, SSD_CONV_DIM), lambda b, c: (b, c, 0)),
                  pl.BlockSpec((1, q, HEAD_PAD), lambda b, c: (b, c, 0)),
                  const((SSD_CONV, SSD_CONV_DIM)),
                  const((1, SSD_CONV_DIM)),
                  const((1, HEAD_PAD)),
                  const((1, HEAD_PAD)),
                  const((HEAD_PAD, MAIN_WIDTH)),
                  const((1, MAIN_WIDTH))],
        out_specs=pl.BlockSpec((1, q, MAIN_WIDTH), lambda b, c: (b, c, 0)),
        out_shape=jax.ShapeDtypeStruct((batch, seq, MAIN_WIDTH), F32),
        scratch_shapes=[pltpu.VMEM((q, SSD_CONV_DIM), F32),
                        pltpu.VMEM((SUBLANES, SSD_CONV_DIM), F32),
                        pltpu.VMEM((q, SSD_CONV_DIM), F32),
                        pltpu.VMEM((3 * q, MAIN_WIDTH), F32),
                        pltpu.VMEM((SSD_STATE, MAIN_WIDTH), F32)],
        compiler_params=_cparams(("parallel", "arbitrary")),
        name="ssd_mixer",
    )(xbc.reshape(batch, seq, SSD_CONV_DIM), dt_raw.reshape(batch, seq, HEAD_PAD),
      conv_w, conv_b.reshape(1, SSD_CONV_DIM), dtb, alog, expand, dskip)


def _sb_kernel(q_ref, k_ref, v_ref, o_ref, *, tile):
    t = tile
    row = lax.broadcasted_iota(jnp.int32, (t, t), 0)
    col = lax.broadcasted_iota(jnp.int32, (t, t), 1)
    strict = row > col
    later = strict.astype(BF16)

    def scores(qb, j):
        kj = k_ref[0, pl.ds(pl.multiple_of(j * t, t), t), :]
        z2 = lax.dot_general(qb, kj, (((1,), (1,)), ((), ())), preferred_element_type=F32)
        neg_abs = lax.bitcast_convert_type(
            lax.bitcast_convert_type(z2, jnp.uint32) | jnp.uint32(0x80000000), F32)
        log_sig = jnp.minimum(z2, 0.0) - jnp.log2(1.0 + jnp.exp2(neg_abs))
        return log_sig, log_sig - z2

    def suffix(log_keep, diagonal):
        if diagonal:
            log_keep = jnp.where(strict, log_keep, 0.0)
        after = jnp.dot(log_keep.astype(BF16), later, preferred_element_type=F32)
        return after, jnp.sum(log_keep, axis=1, keepdims=True)

    def weigh(j, log_sig, after, acc, diagonal):
        vj = v_ref[0, pl.ds(pl.multiple_of(j * t, t), t), :]
        a = jnp.exp2(log_sig + after + acc)
        if diagonal:
            a = jnp.where(strict, a, 0.0)
        return jnp.dot(a.astype(BF16), vj, preferred_element_type=F32)

    def two_blocks(group, diagonal):
        scored = [(scores(qb, j_a), scores(qb, j_b)) for qb, j_a, j_b, _, _, _ in group]
        summed = [(suffix(sa[1], diagonal), suffix(sb[1], False)) for sa, sb in scored]
        out = []
        for (qb, j_a, j_b, use_b, acc, o), (sa, sb), ((aft_a, tot_a), (aft_b, tot_b)) in zip(group, scored, summed):
            acc_a = acc + tot_a
            o_a = o + weigh(j_a, sa[0], aft_a, acc, diagonal)
            o_b = o_a + weigh(j_b, sb[0], aft_b, acc_a, False)
            out.append((jnp.where(use_b, acc_a + tot_b, acc_a), jnp.where(use_b, o_b, o_a)))
        return out

    group = []
    for r in range(SB_SUBBLOCKS):
        i = pl.program_id(2) * SB_SUBBLOCKS + r
        qb = q_ref[0, r * t:(r + 1) * t, :]
        group.append((qb, i, jnp.maximum(i - 1, 0), i >= 1,
                      jnp.zeros((t, 1), F32), jnp.zeros((t, SB_HEAD_DIM), F32)))
    started = [(pl.program_id(2) * SB_SUBBLOCKS + r, g[0], acc, o)
               for r, (g, (acc, o)) in enumerate(zip(group, two_blocks(group, True)))]

    for r, (i, qb, acc, o) in enumerate(started):
        def unfinished(carry, i=i):
            step, acc, _ = carry
            return jnp.logical_and(step < i // 2, jnp.max(acc) > SB_EXHAUSTED_LOG2)

        def body(carry, i=i, qb=qb):
            step, acc, o = carry
            j_a = i - 2 - 2 * step
            (acc, o), = two_blocks([(qb, j_a, jnp.maximum(j_a - 1, 0), j_a >= 1, acc, o)], False)
            return step + 1, acc, o

        _, _, o = lax.while_loop(unfinished, body, (jnp.int32(0), acc, o))
        o_ref[0, r * t:(r + 1) * t, :] = o


def _sb_mixer(qkv, batch, seq, tile=256):
    rows = SB_SUBBLOCKS * tile
    assert seq % rows == 0
    nq = seq // rows
    qkv3 = qkv.reshape(batch, seq, 3 * MAIN_WIDTH)
    return pl.pallas_call(
        functools.partial(_sb_kernel, tile=tile),
        grid=(batch, SB_HEADS, nq),
        in_specs=[pl.BlockSpec((1, rows, SB_HEAD_DIM), lambda b, h, i: (b, i, h)),
                  pl.BlockSpec((1, seq, SB_HEAD_DIM), lambda b, h, i: (b, 0, SB_HEADS + h)),
                  pl.BlockSpec((1, seq, SB_HEAD_DIM), lambda b, h, i: (b, 0, 2 * SB_HEADS + h))],
        out_specs=pl.BlockSpec((1, rows, SB_HEAD_DIM), lambda b, h, i: (b, i, h)),
        out_shape=jax.ShapeDtypeStruct((batch, seq, MAIN_WIDTH), F32),
        compiler_params=_cparams(("parallel", "parallel", "arbitrary")),
        name="stick_breaking",
    )(qkv3, qkv3, qkv3)


def _mem_kernel(qm_ref, kv_ref, o_ref):
    scale = MEM_HEAD_DIM ** -0.5
    for h in range(MEM_HEADS):
        hs = slice(h * MEM_HEAD_DIM, (h + 1) * MEM_HEAD_DIM)
        kh = kv_ref[0, :, hs]
        vh = kv_ref[0, :, MEM_WIDTH + h * MEM_HEAD_DIM:MEM_WIDTH + (h + 1) * MEM_HEAD_DIM]
        s = lax.dot_general(qm_ref[:, hs], kh, (((1,), (1,)), ((), ())),
                            preferred_element_type=F32) * scale
        e = jnp.exp(s - jnp.max(s, axis=-1, keepdims=True))
        p = e * (1.0 / jnp.sum(e, axis=-1, keepdims=True))
        o_ref[:, hs] = jnp.dot(p.astype(BF16), vh, preferred_element_type=F32)


def _mem_attention(q_mem, kv, seq, tm=512):
    m = q_mem.shape[0]
    tiles_per_batch = seq // tm
    return pl.pallas_call(
        _mem_kernel,
        grid=(m // tm,),
        in_specs=[pl.BlockSpec((tm, MEM_WIDTH), lambda i: (i, 0)),
                  pl.BlockSpec((1, MEM_TOKENS, 2 * MEM_WIDTH), lambda i: (i // tiles_per_batch, 0, 0))],
        out_specs=pl.BlockSpec((tm, MEM_WIDTH), lambda i: (i, 0)),
        out_shape=jax.ShapeDtypeStruct((m, MEM_WIDTH), F32),
        compiler_params=_cparams(("parallel",)),
        name="memory_attention",
    )(q_mem, kv)


def kernel(x, mem, mem_norm, ssd_norm, ssd_w_in, ssd_conv_w, ssd_conv_b, ssd_dt_bias, ssd_a_log, ssd_d, ssd_mem_kv, ssd_out_norm, ssd_w_out, sb_norm, sb_w_in, sb_mem_kv, sb_out_norm, sb_w_out, final_norm):
    batch, seq, d = x.shape
    m = batch * seq
    depth = ssd_w_in.shape[0] + sb_w_in.shape[0]
    xf = x.reshape(m, d)
    mem_n = _rmsnorm(mem.reshape(batch * MEM_TOKENS, d), mem_norm, BF16)
    in_norm = lambda layer: (ssd_norm if layer % 2 == 0 else sb_norm)[layer // 2]

    xg, inv = _prep(xf, in_norm(0))
    for layer in range(depth):
        j = layer // 2
        if layer % 2 == 0:
            w_in, w_kv, out_norm, w_out = ssd_w_in, ssd_mem_kv, ssd_out_norm, ssd_w_out
            c0, c1, c2 = SSD_CONV_DIM, SSD_CONV_DIM + SSD_HEADS, SSD_CONV_DIM + SSD_HEADS + MIX_WIDTH
            w_dt = jnp.pad(w_in[j, :, c0:c1], ((0, 0), (0, HEAD_PAD - SSD_HEADS)))[None]
            w_z, lz, z0 = w_in[j:j + 1, :, c1:c2], 0, 0
            w_q, lq, q0 = w_in[j:j + 1, :, c2:], 0, 0
            xbc = _project(xg, w_in, j, 0, c0, F32, inv=inv)
            dt_raw = _project(xg, w_dt, 0, 0, HEAD_PAD, F32, inv=inv)
            main = _ssd_mixer(xbc, dt_raw, ssd_conv_w[j], ssd_conv_b[j], ssd_dt_bias[j],
                              ssd_a_log[j], ssd_d[j], batch, seq)
        else:
            w_in, w_kv, out_norm, w_out = sb_w_in, sb_mem_kv, sb_out_norm, sb_w_out
            c0, c1 = 3 * MAIN_WIDTH, 3 * MAIN_WIDTH + MIX_WIDTH
            w_z, lz, z0 = w_in, j, c0
            w_q, lq, q0 = w_in, j, c1
            qkv = _project(xg, w_in, j, 0, c0, BF16, inv=inv, lead_scale=(MAIN_WIDTH, SB_QUERY_SCALE))
            main = _sb_mixer(qkv, batch, seq)
        q_mem = _project(xg, w_q, lq, q0, MEM_WIDTH, BF16, inv=inv)
        kv = _project(mem_n, w_kv, j, 0, 2 * MEM_WIDTH, BF16).reshape(batch, MEM_TOKENS, 2 * MEM_WIDTH)
        mem_out = _mem_attention(q_mem, kv, seq)
        yg, ssq = _project(xg, w_z, lz, z0, MIX_WIDTH, BF16, inv=inv,
                           gate=(main.reshape(m, MAIN_WIDTH), mem_out, out_norm[j]))
        g_next = in_norm(layer + 1) if layer + 1 < depth else final_norm
        xf, xg, inv = _out_project(yg, ssq, w_out[j].astype(BF16), xf, g_next)

    return _rmsnorm(xf, final_norm, F32).reshape(batch, seq, d)
```

```python
import functools

import jax
import jax.numpy as jnp
from jax import lax
from jax.experimental import pallas as pl
from jax.experimental.pallas import tpu as pltpu

D_MODEL = 2048
MEM_TOKENS = 256
MIX_WIDTH = 2 * D_MODEL
MEM_HEADS = 4
MEM_WIDTH = MIX_WIDTH // 4
MEM_HEAD_DIM = MEM_WIDTH // MEM_HEADS
MAIN_WIDTH = MIX_WIDTH - MEM_WIDTH
SSD_HEAD_DIM = 64
SSD_HEADS = MAIN_WIDTH // SSD_HEAD_DIM
SSD_GROUPS = 8
SSD_HEADS_PER_GROUP = SSD_HEADS // SSD_GROUPS
SSD_STATE = 128
SSD_CONV = 4
SSD_CHUNK = 128
SSD_BC_WIDTH = SSD_GROUPS * SSD_STATE
SSD_CONV_DIM = MAIN_WIDTH + 2 * SSD_BC_WIDTH
SB_HEAD_DIM = 128
SB_HEADS = MAIN_WIDTH // SB_HEAD_DIM
EPS = 1e-6
LOG2E = 1.4426950408889634
SB_QUERY_SCALE = SB_HEAD_DIM ** -0.5 * LOG2E
SB_SUBBLOCKS = 8
SB_EXHAUSTED_LOG2 = -160.0

LANES = 128
SUBLANES = 8
HEAD_PAD = LANES
VMEM_LIMIT = 56 * 1024 * 1024

BF16 = jnp.bfloat16
F32 = jnp.float32


def _cparams(semantics):
    return pltpu.CompilerParams(dimension_semantics=semantics, vmem_limit_bytes=VMEM_LIMIT)


def _rms_kernel(x_ref, g_ref, o_ref):
    x = x_ref[...]
    ms = jnp.mean(x * x, axis=-1, keepdims=True)
    o_ref[...] = (x * lax.rsqrt(ms + EPS) * g_ref[...]).astype(o_ref.dtype)


def _rmsnorm(x, g, out_dtype, tm=512):
    m, d = x.shape
    tm = min(tm, m)
    return pl.pallas_call(
        _rms_kernel,
        grid=(m // tm,),
        in_specs=[pl.BlockSpec((tm, d), lambda i: (i, 0)),
                  pl.BlockSpec((1, d), lambda i: (0, 0))],
        out_specs=pl.BlockSpec((tm, d), lambda i: (i, 0)),
        out_shape=jax.ShapeDtypeStruct((m, d), out_dtype),
        compiler_params=_cparams(("parallel",)),
        name="rmsnorm",
    )(x, g.reshape(1, d))


def _inv_rms(sum_sq, width):
    return lax.rsqrt(sum_sq * (1.0 / width) + EPS)


def _prep_kernel(x_ref, g_ref, xg_ref, inv_ref):
    x = x_ref[...]
    xg_ref[...] = (x * g_ref[...]).astype(BF16)
    inv = _inv_rms(jnp.sum(x * x, axis=-1, keepdims=True), x.shape[-1])
    inv_ref[...] = jnp.broadcast_to(inv, inv_ref.shape)


def _prep(x, g, tm=512):
    m, d = x.shape
    return pl.pallas_call(
        _prep_kernel,
        grid=(m // tm,),
        in_specs=[pl.BlockSpec((tm, d), lambda i: (i, 0)),
                  pl.BlockSpec((1, d), lambda i: (0, 0))],
        out_specs=[pl.BlockSpec((tm, d), lambda i: (i, 0)),
                   pl.BlockSpec((tm, LANES), lambda i: (i, 0))],
        out_shape=[jax.ShapeDtypeStruct((m, d), BF16), jax.ShapeDtypeStruct((m, LANES), F32)],
        compiler_params=_cparams(("parallel",)),
        name="norm_prep",
    )(x, g.reshape(1, d))


def _cast_weight_tile(w_ref, wb_ref):
    @pl.when(pl.program_id(1) == 0)
    def _():
        wb_ref[...] = w_ref[...].astype(BF16)


def _proj_kernel(x_ref, w_ref, o_ref, wb_ref):
    _cast_weight_tile(w_ref, wb_ref)
    o_ref[...] = jnp.dot(x_ref[...], wb_ref[...], preferred_element_type=F32).astype(o_ref.dtype)


def _proj_inv_kernel(x_ref, inv_ref, w_ref, o_ref, wb_ref, *, scaled_tiles, tile_scale):
    _cast_weight_tile(w_ref, wb_ref)
    acc = jnp.dot(x_ref[...], wb_ref[...], preferred_element_type=F32)
    row_scale = inv_ref[:, 0:1]
    if scaled_tiles:
        row_scale = row_scale * jnp.where(pl.program_id(0) < scaled_tiles, tile_scale, 1.0)
    o_ref[...] = (acc * row_scale).astype(o_ref.dtype)


def _gate_kernel(x_ref, inv_ref, w_ref, main_ref, mem_ref, g_ref, yg_ref, ssq_ref, wb_ref, *, main_tiles):
    _cast_weight_tile(w_ref, wb_ref)
    z = jnp.dot(x_ref[...], wb_ref[...], preferred_element_type=F32) * inv_ref[:, 0:1]
    other = jnp.where(pl.program_id(0) < main_tiles, main_ref[...], mem_ref[...])
    y = other * _silu(z)
    yg_ref[...] = (y * g_ref[...]).astype(BF16)
    ssq_ref[...] = jnp.broadcast_to(jnp.sum(y * y, axis=-1, keepdims=True), ssq_ref.shape)


def _project(x, w, layer, col0, n, out_dtype, inv=None, gate=None, lead_scale=None, tm=1024, tn=1024):
    m, k = x.shape
    tm = min(tm, m)
    tn = min(tn, n)
    assert col0 % tn == 0 and n % tn == 0 and m % tm == 0
    j0 = col0 // tn
    x_spec = pl.BlockSpec((tm, k), lambda j, i: (i, 0))
    inv_spec = pl.BlockSpec((tm, LANES), lambda j, i: (i, 0))
    w_spec = pl.BlockSpec((None, k, tn), lambda j, i: (layer, 0, j0 + j))
    tile_spec = pl.BlockSpec((tm, tn), lambda j, i: (i, j))
    common = dict(grid=(n // tn, m // tm), scratch_shapes=[pltpu.VMEM((k, tn), BF16)],
                  compiler_params=_cparams(("parallel", "arbitrary")))
    if gate is not None:
        main, mem, g = gate
        assert main.shape[1] % tn == 0 and mem.shape[1] == tn and main.shape[1] + tn == n
        main_tiles = main.shape[1] // tn
        main_spec = pl.BlockSpec((tm, tn), lambda j, i: (jnp.where(j < main_tiles, i, 0),
                                                         jnp.minimum(j, main_tiles - 1)))
        mem_spec = pl.BlockSpec((tm, tn), lambda j, i: (jnp.where(j < main_tiles, 0, i), 0))
        return pl.pallas_call(
            functools.partial(_gate_kernel, main_tiles=main_tiles),
            in_specs=[x_spec, inv_spec, w_spec, main_spec, mem_spec,
                      pl.BlockSpec((1, tn), lambda j, i: (0, j))],
            out_specs=[tile_spec, pl.BlockSpec((None, tm, LANES), lambda j, i: (j, i, 0))],
            out_shape=[jax.ShapeDtypeStruct((m, n), BF16),
                       jax.ShapeDtypeStruct((n // tn, m, LANES), F32)],
            name="gate_projection", **common,
        )(x, inv, w, main, mem, g.reshape(1, n))
    if inv is not None:
        width, c = lead_scale if lead_scale is not None else (0, 1.0)
        assert width % tn == 0
        body = functools.partial(_proj_inv_kernel, scaled_tiles=width // tn, tile_scale=c)
        return pl.pallas_call(
            body, in_specs=[x_spec, inv_spec, w_spec], out_specs=tile_spec,
            out_shape=jax.ShapeDtypeStruct((m, n), out_dtype), name="projection", **common,
        )(x, inv, w)
    return pl.pallas_call(
        _proj_kernel, in_specs=[x_spec, w_spec], out_specs=tile_spec,
        out_shape=jax.ShapeDtypeStruct((m, n), out_dtype), name="projection_plain", **common,
    )(x, w)


def _out_kernel(yg_ref, ssq_ref, w_ref, res_ref, gn_ref, x_ref, xg_ref, invn_ref, invy_ref, acc_ref):
    j = pl.program_id(1)

    @pl.when(j == 0)
    def _():
        total = ssq_ref[0, :, 0:1]
        for p in range(1, ssq_ref.shape[0]):
            total = total + ssq_ref[p, :, 0:1]
        invy_ref[...] = _inv_rms(total, MIX_WIDTH)
        acc_ref[...] = jnp.zeros_like(acc_ref)

    x_new = res_ref[...] + jnp.dot(yg_ref[...], w_ref[...], preferred_element_type=F32) * invy_ref[...]
    x_ref[...] = x_new
    xg_ref[...] = (x_new * gn_ref[...]).astype(BF16)
    acc_ref[...] += jnp.sum(x_new * x_new, axis=-1, keepdims=True)

    @pl.when(j == pl.num_programs(1) - 1)
    def _():
        invn_ref[...] = jnp.broadcast_to(_inv_rms(acc_ref[...], D_MODEL), invn_ref.shape)


def _out_project(yg, ssq, w, res, g_next, tm=1024, tn=512):
    m, k = yg.shape
    n = w.shape[1]
    parts = ssq.shape[0]
    return pl.pallas_call(
        _out_kernel,
        grid=(m // tm, n // tn),
        in_specs=[pl.BlockSpec((tm, k), lambda i, j: (i, 0)),
                  pl.BlockSpec((parts, tm, LANES), lambda i, j: (0, i, 0)),
                  pl.BlockSpec((k, tn), lambda i, j: (0, j)),
                  pl.BlockSpec((tm, tn), lambda i, j: (i, j)),
                  pl.BlockSpec((1, tn), lambda i, j: (0, j))],
        out_specs=[pl.BlockSpec((tm, tn), lambda i, j: (i, j)),
                   pl.BlockSpec((tm, tn), lambda i, j: (i, j)),
                   pl.BlockSpec((tm, LANES), lambda i, j: (i, 0))],
        out_shape=[jax.ShapeDtypeStruct((m, n), F32), jax.ShapeDtypeStruct((m, n), BF16),
                   jax.ShapeDtypeStruct((m, LANES), F32)],
        scratch_shapes=[pltpu.VMEM((tm, 1), F32), pltpu.VMEM((tm, 1), F32)],
        compiler_params=_cparams(("parallel", "arbitrary")),
        name="out_projection",
    )(yg, ssq, w, res, g_next.reshape(1, n))


def _split3(x):
    p0 = x.astype(BF16)
    r1 = x - p0.astype(F32)
    p1 = r1.astype(BF16)
    p2 = (r1 - p1.astype(F32)).astype(BF16)
    return p0, p1, p2


def _dot_f32_by_01(x, ones_rhs):
    p0, p1, p2 = _split3(x)
    d = functools.partial(jnp.dot, preferred_element_type=F32)
    return d(p0, ones_rhs) + (d(p1, ones_rhs) + d(p2, ones_rhs))


def _dot_01_by_f32(ones_lhs, x):
    p0, p1, p2 = _split3(x)
    d = functools.partial(jnp.dot, preferred_element_type=F32)
    return d(ones_lhs, p0) + (d(ones_lhs, p1) + d(ones_lhs, p2))


def _softplus(x):
    return jnp.maximum(x, 0.0) + jnp.log(1.0 + jnp.exp(-jnp.abs(x)))


def _silu(x):
    return x / (1.0 + jnp.exp(-x))


def _ssd_kernel(xbc_ref, dt_ref, cw_ref, cb_ref, dtb_ref, alog_ref, expand_ref, dskip_ref,
                y_ref, ext_ref, carry_ref, xc_ref, ex_ref, st_ref):
    q = SSD_CHUNK
    hp = SSD_HEAD_DIM
    gw = SSD_HEADS_PER_GROUP * hp

    @pl.when(pl.program_id(1) == 0)
    def _():
        carry_ref[...] = jnp.zeros_like(carry_ref)
        st_ref[...] = jnp.zeros_like(st_ref)

    u = xbc_ref[0]
    shifted = None
    for tap in range(SSD_CONV - 1):
        part = cw_ref[tap:tap + 1, :] * u
        if shifted is not None:
            part = part + shifted
        ext_ref[...] = pltpu.roll(part, 1, axis=0)
        last_row = ext_ref[0:1, :]
        ext_ref[0:1, :] = carry_ref[tap:tap + 1, :]
        carry_ref[tap:tap + 1, :] = last_row
        shifted = ext_ref[...]
    conv = cb_ref[...] + cw_ref[SSD_CONV - 1:SSD_CONV, :] * u + shifted
    xc_ref[...] = _silu(conv)

    dt = _softplus(dt_ref[0] + dtb_ref[...])
    d_a = dt * (-jnp.exp(alog_ref[...]))
    row = lax.broadcasted_iota(jnp.int32, (q, q), 0)
    col = lax.broadcasted_iota(jnp.int32, (q, q), 1)
    causal = row >= col
    cum = _dot_01_by_f32(causal.astype(BF16), d_a)
    cum_t = cum.T
    cum_end = cum[q - 1:q, :]
    to_end = jnp.exp(cum_end - cum) * dt
    ecum = jnp.exp(cum)

    expand = expand_ref[...]
    ex_ref[0:2 * q, :] = jnp.dot(jnp.concatenate([dt, to_end], axis=0).astype(BF16), expand,
                                 preferred_element_type=F32)
    ecum_hi = ecum.astype(BF16)
    ecum_lo = (ecum - ecum_hi.astype(F32)).astype(BF16)
    ex_ref[2 * q:3 * q, :] = (jnp.dot(ecum_hi, expand, preferred_element_type=F32)
                              + jnp.dot(ecum_lo, expand, preferred_element_type=F32))

    lane = lax.broadcasted_iota(jnp.int32, (q, LANES), 1)
    low_half = lane < hp

    for g in range(SSD_GROUPS):
        bg = xc_ref[:, MAIN_WIDTH + g * SSD_STATE:MAIN_WIDTH + (g + 1) * SSD_STATE]
        cg = xc_ref[:, MAIN_WIDTH + SSD_BC_WIDTH + g * SSD_STATE:
                    MAIN_WIDTH + SSD_BC_WIDTH + (g + 1) * SSD_STATE]
        cg16 = cg.astype(BF16)
        cb = lax.dot_general(cg16, bg.astype(BF16), (((1,), (1,)), ((), ())),
                             preferred_element_type=F32)
        gs = slice(g * gw, (g + 1) * gw)
        y_off = jnp.dot(cg16, st_ref[:, gs].astype(BF16), preferred_element_type=F32)

        for pair in range(SSD_HEADS_PER_GROUP // 2):
            ws = []
            for r in (2 * pair, 2 * pair + 1):
                h = g * SSD_HEADS_PER_GROUP + r
                seg = cum[:, h:h + 1] - cum_t[h:h + 1, :]
                ws.append(cb * jnp.exp(jnp.where(causal, seg, -jnp.inf)))
            w2 = jnp.concatenate(ws, axis=1).astype(BF16)
            ps = slice(g * gw + pair * LANES, g * gw + (pair + 1) * LANES)
            xs = xc_ref[:, ps]
            xdt = xs * ex_ref[0:q, ps]
            rhs = jnp.concatenate([jnp.where(low_half, xdt, 0.0),
                                   jnp.where(low_half, 0.0, xdt)], axis=0).astype(BF16)
            y_diag = jnp.dot(w2, rhs, preferred_element_type=F32)
            yo = y_off[:, pair * LANES:(pair + 1) * LANES]
            y_ref[0, :, ps] = y_diag + ex_ref[2 * q:3 * q, ps] * yo + xs * dskip_ref[:, ps]

        xte = (xc_ref[:, gs] * ex_ref[q:2 * q, gs]).astype(BF16)
        upd = jnp.dot(bg.T.astype(BF16), xte, preferred_element_type=F32)
        st_ref[:, gs] = st_ref[:, gs] * ex_ref[3 * q - 1:3 * q, gs] + upd


def _ssd_mixer(xbc, dt_raw, conv_w, conv_b, dt_bias, a_log, d_skip, batch, seq):
    q = SSD_CHUNK
    nchunks = seq // q
    pad = HEAD_PAD - SSD_HEADS
    dtb = jnp.pad(dt_bias, (0, pad)).reshape(1, HEAD_PAD)
    alog = jnp.pad(a_log, (0, pad)).reshape(1, HEAD_PAD)
    head_of_lane = jnp.arange(MAIN_WIDTH) // SSD_HEAD_DIM
    expand = (jnp.arange(HEAD_PAD)[:, None] == head_of_lane[None, :]).astype(BF16)
    dskip = jnp.repeat(d_skip, SSD_HEAD_DIM).reshape(1, MAIN_WIDTH)
    const = lambda shape: pl.BlockSpec(shape, lambda b, c: (0,) * len(shape))
    return pl.pallas_call(
        _ssd_kernel,
        grid=(batch, nchunks),
        in_specs=[pl.BlockSpec((1, q, SSD_CONV_DIM), lambda b, c: (b, c, 0)),
                  pl.BlockSpec((1, q, HEAD_PAD), lambda b, c: (b, c, 0)),
                  const((SSD_CONV, SSD_CONV_DIM)),
                  const((1, SSD_CONV_DIM)),
                  const((1, HEAD_PAD)),
                  const((1, HEAD_PAD)),
                  const((HEAD_PAD, MAIN_WIDTH)),
                  const((1, MAIN_WIDTH))],
        out_specs=pl.BlockSpec((1, q, MAIN_WIDTH), lambda b, c: (b, c, 0)),
        out_shape=jax.ShapeDtypeStruct((batch, seq, MAIN_WIDTH), F32),
        scratch_shapes=[pltpu.VMEM((q, SSD_CONV_DIM), F32),
                        pltpu.VMEM((SUBLANES, SSD_CONV_DIM), F32),
                        pltpu.VMEM((q, SSD_CONV_DIM), F32),
                        pltpu.VMEM((3 * q, MAIN_WIDTH), F32),
                        pltpu.VMEM((SSD_STATE, MAIN_WIDTH), F32)],
        compiler_params=_cparams(("parallel", "arbitrary")),
        name="ssd_mixer",
    )(xbc.reshape(batch, seq, SSD_CONV_DIM), dt_raw.reshape(batch, seq, HEAD_PAD),
      conv_w, conv_b.reshape(1, SSD_CONV_DIM), dtb, alog, expand, dskip)


def _sb_kernel(q_ref, k_ref, v_ref, o_ref, *, tile):
    t = tile
    row = lax.broadcasted_iota(jnp.int32, (t, t), 0)
    col = lax.broadcasted_iota(jnp.int32, (t, t), 1)
    strict = row > col
    later = strict.astype(BF16)

    def scores(qb, j):
        kj = k_ref[0, pl.ds(pl.multiple_of(j * t, t), t), :]
        z2 = lax.dot_general(qb, kj, (((1,), (1,)), ((), ())), preferred_element_type=F32)
        neg_abs = lax.bitcast_convert_type(
            lax.bitcast_convert_type(z2, jnp.uint32) | jnp.uint32(0x80000000), F32)
        log_sig = jnp.minimum(z2, 0.0) - jnp.log2(1.0 + jnp.exp2(neg_abs))
        return log_sig, log_sig - z2

    def suffix(log_keep, diagonal):
        if diagonal:
            log_keep = jnp.where(strict, log_keep, 0.0)
        after = jnp.dot(log_keep.astype(BF16), later, preferred_element_type=F32)
        return after, jnp.sum(log_keep, axis=1, keepdims=True)

    def weigh(j, log_sig, after, acc, diagonal):
        vj = v_ref[0, pl.ds(pl.multiple_of(j * t, t), t), :]
        a = jnp.exp2(log_sig + after + acc)
        if diagonal:
            a = jnp.where(strict, a, 0.0)
        return jnp.dot(a.astype(BF16), vj, preferred_element_type=F32)

    def two_blocks(group, diagonal):
        scored = [(scores(qb, j_a), scores(qb, j_b)) for qb, j_a, j_b, _, _, _ in group]
        summed = [(suffix(sa[1], diagonal), suffix(sb[1], False)) for sa, sb in scored]
        out = []
        for (qb, j_a, j_b, use_b, acc, o), (sa, sb), ((aft_a, tot_a), (aft_b, tot_b)) in zip(group, scored, summed):
            acc_a = acc + tot_a
            o_a = o + weigh(j_a, sa[0], aft_a, acc, diagonal)
            o_b = o_a + weigh(j_b, sb[0], aft_b, acc_a, False)
            out.append((jnp.where(use_b, acc_a + tot_b, acc_a), jnp.where(use_b, o_b, o_a)))
        return out

    group = []
    for r in range(SB_SUBBLOCKS):
        i = pl.program_id(2) * SB_SUBBLOCKS + r
        qb = q_ref[0, r * t:(r + 1) * t, :]
        group.append((qb, i, jnp.maximum(i - 1, 0), i >= 1,
                      jnp.zeros((t, 1), F32), jnp.zeros((t, SB_HEAD_DIM), F32)))
    started = [(pl.program_id(2) * SB_SUBBLOCKS + r, g[0], acc, o)
               for r, (g, (acc, o)) in enumerate(zip(group, two_blocks(group, True)))]

    for r, (i, qb, acc, o) in enumerate(started):
        def unfinished(carry, i=i):
            step, acc, _ = carry
            return jnp.logical_and(step < i // 2, jnp.max(acc) > SB_EXHAUSTED_LOG2)

        def body(carry, i=i, qb=qb):
            step, acc, o = carry
            j_a = i - 2 - 2 * step
            (acc, o), = two_blocks([(qb, j_a, jnp.maximum(j_a - 1, 0), j_a >= 1, acc, o)], False)
            return step + 1, acc, o

        _, _, o = lax.while_loop(unfinished, body, (jnp.int32(0), acc, o))
        o_ref[0, r * t:(r + 1) * t, :] = o


def _sb_mixer(qkv, batch, seq, tile=256):
    rows = SB_SUBBLOCKS * tile
    assert seq % rows == 0
    nq = seq // rows
    qkv3 = qkv.reshape(batch, seq, 3 * MAIN_WIDTH)
    return pl.pallas_call(
        functools.partial(_sb_kernel, tile=tile),
        grid=(batch, SB_HEADS, nq),
        in_specs=[pl.BlockSpec((1, rows, SB_HEAD_DIM), lambda b, h, i: (b, i, h)),
                  pl.BlockSpec((1, seq, SB_HEAD_DIM), lambda b, h, i: (b, 0, SB_HEADS + h)),
                  pl.BlockSpec((1, seq, SB_HEAD_DIM), lambda b, h, i: (b, 0, 2 * SB_HEADS + h))],
        out_specs=pl.BlockSpec((1, rows, SB_HEAD_DIM), lambda b, h, i: (b, i, h)),
        out_shape=jax.ShapeDtypeStruct((batch, seq, MAIN_WIDTH), F32),
        compiler_params=_cparams(("parallel", "parallel", "arbitrary")),
        name="stick_breaking",
    )(qkv3, qkv3, qkv3)


def _mem_kernel(qm_ref, kv_ref, o_ref):
    scale = MEM_HEAD_DIM ** -0.5
    for h in range(MEM_HEADS):
        hs = slice(h * MEM_HEAD_DIM, (h + 1) * MEM_HEAD_DIM)
        kh = kv_ref[0, :, hs]
        vh = kv_ref[0, :, MEM_WIDTH + h * MEM_HEAD_DIM:MEM_WIDTH + (h + 1) * MEM_HEAD_DIM]
        s = lax.dot_general(qm_ref[:, hs], kh, (((1,), (1,)), ((), ())),
                            preferred_element_type=F32) * scale
        e = jnp.exp(s - jnp.max(s, axis=-1, keepdims=True))
        p = e * (1.0 / jnp.sum(e, axis=-1, keepdims=True))
        o_ref[:, hs] = jnp.dot(p.astype(BF16), vh, preferred_element_type=F32)


def _mem_attention(q_mem, kv, seq, tm=512):
    m = q_mem.shape[0]
    tiles_per_batch = seq // tm
    return pl.pallas_call(
        _mem_kernel,
        grid=(m // tm,),
        in_specs=[pl.BlockSpec((tm, MEM_WIDTH), lambda i: (i, 0)),
                  pl.BlockSpec((1, MEM_TOKENS, 2 * MEM_WIDTH), lambda i: (i // tiles_per_batch, 0, 0))],
        out_specs=pl.BlockSpec((tm, MEM_WIDTH), lambda i: (i, 0)),
        out_shape=jax.ShapeDtypeStruct((m, MEM_WIDTH), F32),
        compiler_params=_cparams(("parallel",)),
        name="memory_attention",
    )(q_mem, kv)


def kernel(x, mem, mem_norm, ssd_norm, ssd_w_in, ssd_conv_w, ssd_conv_b, ssd_dt_bias, ssd_a_log, ssd_d, ssd_mem_kv, ssd_out_norm, ssd_w_out, sb_norm, sb_w_in, sb_mem_kv, sb_out_norm, sb_w_out, final_norm):
    batch, seq, d = x.shape
    m = batch * seq
    depth = ssd_w_in.shape[0] + sb_w_in.shape[0]
    xf = x.reshape(m, d)
    mem_n = _rmsnorm(mem.reshape(batch * MEM_TOKENS, d), mem_norm, BF16)
    in_norm = lambda layer: (ssd_norm if layer % 2 == 0 else sb_norm)[layer // 2]

    xg, inv = _prep(xf, in_norm(0))
    for layer in range(depth):
        j = layer // 2
        if layer % 2 == 0:
            w_in, w_kv, out_norm, w_out = ssd_w_in, ssd_mem_kv, ssd_out_norm, ssd_w_out
            c0, c1, c2 = SSD_CONV_DIM, SSD_CONV_DIM + SSD_HEADS, SSD_CONV_DIM + SSD_HEADS + MIX_WIDTH
            w_dt = jnp.pad(w_in[j, :, c0:c1], ((0, 0), (0, HEAD_PAD - SSD_HEADS)))[None]
            w_z, lz, z0 = w_in[j:j + 1, :, c1:c2], 0, 0
            w_q, lq, q0 = w_in[j:j + 1, :, c2:], 0, 0
            xbc = _project(xg, w_in, j, 0, c0, F32, inv=inv)
            dt_raw = _project(xg, w_dt, 0, 0, HEAD_PAD, F32, inv=inv)
            main = _ssd_mixer(xbc, dt_raw, ssd_conv_w[j], ssd_conv_b[j], ssd_dt_bias[j],
                              ssd_a_log[j], ssd_d[j], batch, seq)
        else:
            w_in, w_kv, out_norm, w_out = sb_w_in, sb_mem_kv, sb_out_norm, sb_w_out
            c0, c1 = 3 * MAIN_WIDTH, 3 * MAIN_WIDTH + MIX_WIDTH
            w_z, lz, z0 = w_in, j, c0
            w_q, lq, q0 = w_in, j, c1
            qkv = _project(xg, w_in, j, 0, c0, BF16, inv=inv, lead_scale=(MAIN_WIDTH, SB_QUERY_SCALE))
            main = _sb_mixer(qkv, batch, seq)
        q_mem = _project(xg, w_q, lq, q0, MEM_WIDTH, BF16, inv=inv)
        kv = _project(mem_n, w_kv, j, 0, 2 * MEM_WIDTH, BF16).reshape(batch, MEM_TOKENS, 2 * MEM_WIDTH)
        mem_out = _mem_attention(q_mem, kv, seq)
        yg, ssq = _project(xg, w_z, lz, z0, MIX_WIDTH, BF16, inv=inv,
                           gate=(main.reshape(m, MAIN_WIDTH), mem_out, out_norm[j]))
        g_next = in_norm(layer + 1) if layer + 1 < depth else final_norm
        xf, xg, inv = _out_project(yg, ssq, w_out[j].astype(BF16), xf, g_next)

    return _rmsnorm(xf, final_norm, F32).reshape(batch, seq, d)
```

```python
import functools

import jax
import jax.numpy as jnp
from jax import lax
from jax.experimental import pallas as pl
from jax.experimental.pallas import tpu as pltpu

D_MODEL = 2048
MEM_TOKENS = 256
MIX_WIDTH = 2 * D_MODEL
MEM_HEADS = 4
MEM_WIDTH = MIX_WIDTH // 4
MEM_HEAD_DIM = MEM_WIDTH // MEM_HEADS
MAIN_WIDTH = MIX_WIDTH - MEM_WIDTH
SSD_HEAD_DIM = 64
SSD_HEADS = MAIN_WIDTH // SSD_HEAD_DIM
SSD_GROUPS = 8
SSD_HEADS_PER_GROUP = SSD_HEADS // SSD_GROUPS
SSD_STATE = 128
SSD_CONV = 4
SSD_CHUNK = 128
SSD_BC_WIDTH = SSD_GROUPS * SSD_STATE
SSD_CONV_DIM = MAIN_WIDTH + 2 * SSD_BC_WIDTH
SB_HEAD_DIM = 128
SB_HEADS = MAIN_WIDTH // SB_HEAD_DIM
EPS = 1e-6
LOG2E = 1.4426950408889634
SB_QUERY_SCALE = SB_HEAD_DIM ** -0.5 * LOG2E
SB_SUBBLOCKS = 16
SB_EXHAUSTED_LOG2 = -160.0

LANES = 128
SUBLANES = 8
HEAD_PAD = LANES
VMEM_LIMIT = 56 * 1024 * 1024

BF16 = jnp.bfloat16
F32 = jnp.float32


def _cparams(semantics):
    return pltpu.CompilerParams(dimension_semantics=semantics, vmem_limit_bytes=VMEM_LIMIT)


def _rms_kernel(x_ref, g_ref, o_ref):
    x = x_ref[...]
    ms = jnp.mean(x * x, axis=-1, keepdims=True)
    o_ref[...] = (x * lax.rsqrt(ms + EPS) * g_ref[...]).astype(o_ref.dtype)


def _rmsnorm(x, g, out_dtype, tm=512):
    m, d = x.shape
    tm = min(tm, m)
    return pl.pallas_call(
        _rms_kernel,
        grid=(m // tm,),
        in_specs=[pl.BlockSpec((tm, d), lambda i: (i, 0)),
                  pl.BlockSpec((1, d), lambda i: (0, 0))],
        out_specs=pl.BlockSpec((tm, d), lambda i: (i, 0)),
        out_shape=jax.ShapeDtypeStruct((m, d), out_dtype),
        compiler_params=_cparams(("parallel",)),
        name="rmsnorm",
    )(x, g.reshape(1, d))


def _inv_rms(sum_sq, width):
    return lax.rsqrt(sum_sq * (1.0 / width) + EPS)


def _prep_kernel(x_ref, g_ref, xg_ref, inv_ref):
    x = x_ref[...]
    xg_ref[...] = (x * g_ref[...]).astype(BF16)
    inv = _inv_rms(jnp.sum(x * x, axis=-1, keepdims=True), x.shape[-1])
    inv_ref[...] = jnp.broadcast_to(inv, inv_ref.shape)


def _prep(x, g, tm=512):
    m, d = x.shape
    return pl.pallas_call(
        _prep_kernel,
        grid=(m // tm,),
        in_specs=[pl.BlockSpec((tm, d), lambda i: (i, 0)),
                  pl.BlockSpec((1, d), lambda i: (0, 0))],
        out_specs=[pl.BlockSpec((tm, d), lambda i: (i, 0)),
                   pl.BlockSpec((tm, LANES), lambda i: (i, 0))],
        out_shape=[jax.ShapeDtypeStruct((m, d), BF16), jax.ShapeDtypeStruct((m, LANES), F32)],
        compiler_params=_cparams(("parallel",)),
        name="norm_prep",
    )(x, g.reshape(1, d))


def _cast_weight_tile(w_ref, wb_ref):
    @pl.when(pl.program_id(1) == 0)
    def _():
        wb_ref[...] = w_ref[...].astype(BF16)


def _proj_kernel(x_ref, w_ref, o_ref, wb_ref):
    _cast_weight_tile(w_ref, wb_ref)
    o_ref[...] = jnp.dot(x_ref[...], wb_ref[...], preferred_element_type=F32).astype(o_ref.dtype)


def _proj_inv_kernel(x_ref, inv_ref, w_ref, o_ref, wb_ref, *, scaled_tiles, tile_scale):
    _cast_weight_tile(w_ref, wb_ref)
    acc = jnp.dot(x_ref[...], wb_ref[...], preferred_element_type=F32)
    row_scale = inv_ref[:, 0:1]
    if scaled_tiles:
        row_scale = row_scale * jnp.where(pl.program_id(0) < scaled_tiles, tile_scale, 1.0)
    o_ref[...] = (acc * row_scale).astype(o_ref.dtype)


def _gate_kernel(x_ref, inv_ref, w_ref, main_ref, mem_ref, g_ref, yg_ref, ssq_ref, wb_ref, *, main_tiles):
    _cast_weight_tile(w_ref, wb_ref)
    z = jnp.dot(x_ref[...], wb_ref[...], preferred_element_type=F32) * inv_ref[:, 0:1]
    other = jnp.where(pl.program_id(0) < main_tiles, main_ref[...], mem_ref[...])
    y = other * _silu(z)
    yg_ref[...] = (y * g_ref[...]).astype(BF16)
    ssq_ref[...] = jnp.broadcast_to(jnp.sum(y * y, axis=-1, keepdims=True), ssq_ref.shape)


def _project(x, w, layer, col0, n, out_dtype, inv=None, gate=None, lead_scale=None, tm=1024, tn=1024):
    m, k = x.shape
    tm = min(tm, m)
    tn = min(tn, n)
    assert col0 % tn == 0 and n % tn == 0 and m % tm == 0
    j0 = col0 // tn
    x_spec = pl.BlockSpec((tm, k), lambda j, i: (i, 0))
    inv_spec = pl.BlockSpec((tm, LANES), lambda j, i: (i, 0))
    w_spec = pl.BlockSpec((None, k, tn), lambda j, i: (layer, 0, j0 + j))
    tile_spec = pl.BlockSpec((tm, tn), lambda j, i: (i, j))
    common = dict(grid=(n // tn, m // tm), scratch_shapes=[pltpu.VMEM((k, tn), BF16)],
                  compiler_params=_cparams(("parallel", "arbitrary")))
    if gate is not None:
        main, mem, g = gate
        assert main.shape[1] % tn == 0 and mem.shape[1] == tn and main.shape[1] + tn == n
        main_tiles = main.shape[1] // tn
        main_spec = pl.BlockSpec((tm, tn), lambda j, i: (jnp.where(j < main_tiles, i, 0),
                                                         jnp.minimum(j, main_tiles - 1)))
        mem_spec = pl.BlockSpec((tm, tn), lambda j, i: (jnp.where(j < main_tiles, 0, i), 0))
        return pl.pallas_call(
            functools.partial(_gate_kernel, main_tiles=main_tiles),
            in_specs=[x_spec, inv_spec, w_spec, main_spec, mem_spec,
                      pl.BlockSpec((1, tn), lambda j, i: (0, j))],
            out_specs=[tile_spec, pl.BlockSpec((None, tm, LANES), lambda j, i: (j, i, 0))],
            out_shape=[jax.ShapeDtypeStruct((m, n), BF16),
                       jax.ShapeDtypeStruct((n // tn, m, LANES), F32)],
            name="gate_projection", **common,
        )(x, inv, w, main, mem, g.reshape(1, n))
    if inv is not None:
        width, c = lead_scale if lead_scale is not None else (0, 1.0)
        assert width % tn == 0
        body = functools.partial(_proj_inv_kernel, scaled_tiles=width // tn, tile_scale=c)
        return pl.pallas_call(
            body, in_specs=[x_spec, inv_spec, w_spec], out_specs=tile_spec,
            out_shape=jax.ShapeDtypeStruct((m, n), out_dtype), name="projection", **common,
        )(x, inv, w)
    return pl.pallas_call(
        _proj_kernel, in_specs=[x_spec, w_spec], out_specs=tile_spec,
        out_shape=jax.ShapeDtypeStruct((m, n), out_dtype), name="projection_plain", **common,
    )(x, w)


def _out_kernel(yg_ref, ssq_ref, w_ref, res_ref, gn_ref, x_ref, xg_ref, invn_ref, invy_ref, acc_ref):
    j = pl.program_id(1)

    @pl.when(j == 0)
    def _():
        total = ssq_ref[0, :, 0:1]
        for p in range(1, ssq_ref.shape[0]):
            total = total + ssq_ref[p, :, 0:1]
        invy_ref[...] = _inv_rms(total, MIX_WIDTH)
        acc_ref[...] = jnp.zeros_like(acc_ref)

    x_new = res_ref[...] + jnp.dot(yg_ref[...], w_ref[...], preferred_element_type=F32) * invy_ref[...]
    x_ref[...] = x_new
    xg_ref[...] = (x_new * gn_ref[...]).astype(BF16)
    acc_ref[...] += jnp.sum(x_new * x_new, axis=-1, keepdims=True)

    @pl.when(j == pl.num_programs(1) - 1)
    def _():
        invn_ref[...] = jnp.broadcast_to(_inv_rms(acc_ref[...], D_MODEL), invn_ref.shape)


def _out_project(yg, ssq, w, res, g_next, tm=1024, tn=512):
    m, k = yg.shape
    n = w.shape[1]
    parts = ssq.shape[0]
    return pl.pallas_call(
        _out_kernel,
        grid=(m // tm, n // tn),
        in_specs=[pl.BlockSpec((tm, k), lambda i, j: (i, 0)),
                  pl.BlockSpec((parts, tm, LANES), lambda i, j: (0, i, 0)),
                  pl.BlockSpec((k, tn), lambda i, j: (0, j)),
                  pl.BlockSpec((tm, tn), lambda i, j: (i, j)),
                  pl.BlockSpec((1, tn), lambda i, j: (0, j))],
        out_specs=[pl.BlockSpec((tm, tn), lambda i, j: (i, j)),
                   pl.BlockSpec((tm, tn), lambda i, j: (i, j)),
                   pl.BlockSpec((tm, LANES), lambda i, j: (i, 0))],
        out_shape=[jax.ShapeDtypeStruct((m, n), F32), jax.ShapeDtypeStruct((m, n), BF16),
                   jax.ShapeDtypeStruct((m, LANES), F32)],
        scratch_shapes=[pltpu.VMEM((tm, 1), F32), pltpu.VMEM((tm, 1), F32)],
        compiler_params=_cparams(("parallel", "arbitrary")),
        name="out_projection",
    )(yg, ssq, w, res, g_next.reshape(1, n))


def _split3(x):
    p0 = x.astype(BF16)
    r1 = x - p0.astype(F32)
    p1 = r1.astype(BF16)
    p2 = (r1 - p1.astype(F32)).astype(BF16)
    return p0, p1, p2


def _dot_f32_by_01(x, ones_rhs):
    p0, p1, p2 = _split3(x)
    d = functools.partial(jnp.dot, preferred_element_type=F32)
    return d(p0, ones_rhs) + (d(p1, ones_rhs) + d(p2, ones_rhs))


def _dot_01_by_f32(ones_lhs, x):
    p0, p1, p2 = _split3(x)
    d = functools.partial(jnp.dot, preferred_element_type=F32)
    return d(ones_lhs, p0) + (d(ones_lhs, p1) + d(ones_lhs, p2))


def _softplus(x):
    return jnp.maximum(x, 0.0) + jnp.log(1.0 + jnp.exp(-jnp.abs(x)))


def _silu(x):
    return x / (1.0 + jnp.exp(-x))


def _ssd_kernel(xbc_ref, dt_ref, cw_ref, cb_ref, dtb_ref, alog_ref, expand_ref, dskip_ref,
                y_ref, ext_ref, carry_ref, xc_ref, ex_ref, st_ref):
    q = SSD_CHUNK
    hp = SSD_HEAD_DIM
    gw = SSD_HEADS_PER_GROUP * hp

    @pl.when(pl.program_id(1) == 0)
    def _():
        carry_ref[...] = jnp.zeros_like(carry_ref)
        st_ref[...] = jnp.zeros_like(st_ref)

    u = xbc_ref[0]
    shifted = None
    for tap in range(SSD_CONV - 1):
        part = cw_ref[tap:tap + 1, :] * u
        if shifted is not None:
            part = part + shifted
        ext_ref[...] = pltpu.roll(part, 1, axis=0)
        last_row = ext_ref[0:1, :]
        ext_ref[0:1, :] = carry_ref[tap:tap + 1, :]
        carry_ref[tap:tap + 1, :] = last_row
        shifted = ext_ref[...]
    conv = cb_ref[...] + cw_ref[SSD_CONV - 1:SSD_CONV, :] * u + shifted
    xc_ref[...] = _silu(conv)

    dt = _softplus(dt_ref[0] + dtb_ref[...])
    d_a = dt * (-jnp.exp(alog_ref[...]))
    row = lax.broadcasted_iota(jnp.int32, (q, q), 0)
    col = lax.broadcasted_iota(jnp.int32, (q, q), 1)
    causal = row >= col
    cum = _dot_01_by_f32(causal.astype(BF16), d_a)
    cum_t = cum.T
    cum_end = cum[q - 1:q, :]
    to_end = jnp.exp(cum_end - cum) * dt
    ecum = jnp.exp(cum)

    expand = expand_ref[...]
    ex_ref[0:2 * q, :] = jnp.dot(jnp.concatenate([dt, to_end], axis=0).astype(BF16), expand,
                                 preferred_element_type=F32)
    ecum_hi = ecum.astype(BF16)
    ecum_lo = (ecum - ecum_hi.astype(F32)).astype(BF16)
    ex_ref[2 * q:3 * q, :] = (jnp.dot(ecum_hi, expand, preferred_element_type=F32)
                              + jnp.dot(ecum_lo, expand, preferred_element_type=F32))

    lane = lax.broadcasted_iota(jnp.int32, (q, LANES), 1)
    low_half = lane < hp

    for g in range(SSD_GROUPS):
        bg = xc_ref[:, MAIN_WIDTH + g * SSD_STATE:MAIN_WIDTH + (g + 1) * SSD_STATE]
        cg = xc_ref[:, MAIN_WIDTH + SSD_BC_WIDTH + g * SSD_STATE:
                    MAIN_WIDTH + SSD_BC_WIDTH + (g + 1) * SSD_STATE]
        cg16 = cg.astype(BF16)
        cb = lax.dot_general(cg16, bg.astype(BF16), (((1,), (1,)), ((), ())),
                             preferred_element_type=F32)
        gs = slice(g * gw, (g + 1) * gw)
        y_off = jnp.dot(cg16, st_ref[:, gs].astype(BF16), preferred_element_type=F32)

        for pair in range(SSD_HEADS_PER_GROUP // 2):
            ws = []
            for r in (2 * pair, 2 * pair + 1):
                h = g * SSD_HEADS_PER_GROUP + r
                seg = cum[:, h:h + 1] - cum_t[h:h + 1, :]
                ws.append(cb * jnp.exp(jnp.where(causal, seg, -jnp.inf)))
            w2 = jnp.concatenate(ws, axis=1).astype(BF16)
            ps = slice(g * gw + pair * LANES, g * gw + (pair + 1) * LANES)
            xs = xc_ref[:, ps]
            xdt = xs * ex_ref[0:q, ps]
            rhs = jnp.concatenate([jnp.where(low_half, xdt, 0.0),
                                   jnp.where(low_half, 0.0, xdt)], axis=0).astype(BF16)
            y_diag = jnp.dot(w2, rhs, preferred_element_type=F32)
            yo = y_off[:, pair * LANES:(pair + 1) * LANES]
            y_ref[0, :, ps] = y_diag + ex_ref[2 * q:3 * q, ps] * yo + xs * dskip_ref[:, ps]

        xte = (xc_ref[:, gs] * ex_ref[q:2 * q, gs]).astype(BF16)
        upd = jnp.dot(bg.T.astype(BF16), xte, preferred_element_type=F32)
        st_ref[:, gs] = st_ref[:, gs] * ex_ref[3 * q - 1:3 * q, gs] + upd


def _ssd_mixer(xbc, dt_raw, conv_w, conv_b, dt_bias, a_log, d_skip, batch, seq):
    q = SSD_CHUNK
    nchunks = seq // q
    pad = HEAD_PAD - SSD_HEADS
    dtb = jnp.pad(dt_bias, (0, pad)).reshape(1, HEAD_PAD)
    alog = jnp.pad(a_log, (0, pad)).reshape(1, HEAD_PAD)
    head_of_lane = jnp.arange(MAIN_WIDTH) // SSD_HEAD_DIM
    expand = (jnp.arange(HEAD_PAD)[:, None] == head_of_lane[None, :]).astype(BF16)
    dskip = jnp.repeat(d_skip, SSD_HEAD_DIM).reshape(1, MAIN_WIDTH)
    const = lambda shape: pl.BlockSpec(shape, lambda b, c: (0,) * len(shape))
    return pl.pallas_call(
        _ssd_kernel,
        grid=(batch, nchunks),
        in_specs=[pl.BlockSpec((1, q, SSD_CONV_DIM), lambda b, c: (b, c, 0)),
                  pl.BlockSpec((1, q, HEAD_PAD), lambda b, c: (b, c, 0)),
                  const((SSD_CONV, SSD_CONV_DIM)),
                  const((1, SSD_CONV_DIM)),
                  const((1, HEAD_PAD)),
                  const((1, HEAD_PAD)),
                  const((HEAD_PAD, MAIN_WIDTH)),
                  const((1, MAIN_WIDTH))],
        out_specs=pl.BlockSpec((1, q, MAIN_WIDTH), lambda b, c: (b, c, 0)),
        out_shape=jax.ShapeDtypeStruct((batch, seq, MAIN_WIDTH), F32),
        scratch_shapes=[pltpu.VMEM((q, SSD_CONV_DIM), F32),
                        pltpu.VMEM((SUBLANES, SSD_CONV_DIM), F32),
                        pltpu.VMEM((q, SSD_CONV_DIM), F32),
                        pltpu.VMEM((3 * q, MAIN_WIDTH), F32),
                        pltpu.VMEM((SSD_STATE, MAIN_WIDTH), F32)],
        compiler_params=_cparams(("parallel", "arbitrary")),
        name="ssd_mixer",
    )(xbc.reshape(batch, seq, SSD_CONV_DIM), dt_raw.reshape(batch, seq, HEAD_PAD),
      conv_w, conv_b.reshape(1, SSD_CONV_DIM), dtb, alog, expand, dskip)


def _sb_kernel(q_ref, k_ref, v_ref, o_ref, *, tile):
    t = tile
    row = lax.broadcasted_iota(jnp.int32, (t, t), 0)
    col = lax.broadcasted_iota(jnp.int32, (t, t), 1)
    strict = row > col
    later = strict.astype(BF16)

    def scores(qb, j):
        kj = k_ref[0, pl.ds(pl.multiple_of(j * t, t), t), :]
        z2 = lax.dot_general(qb, kj, (((1,), (1,)), ((), ())), preferred_element_type=F32)
        neg_abs = lax.bitcast_convert_type(
            lax.bitcast_convert_type(z2, jnp.uint32) | jnp.uint32(0x80000000), F32)
        log_sig = jnp.minimum(z2, 0.0) - jnp.log2(1.0 + jnp.exp2(neg_abs))
        return log_sig, log_sig - z2

    def suffix(log_keep, diagonal):
        if diagonal:
            log_keep = jnp.where(strict, log_keep, 0.0)
        after = jnp.dot(log_keep.astype(BF16), later, preferred_element_type=F32)
        return after, jnp.sum(log_keep, axis=1, keepdims=True)

    def weigh(j, log_sig, after, acc, diagonal):
        vj = v_ref[0, pl.ds(pl.multiple_of(j * t, t), t), :]
        a = jnp.exp2(log_sig + after + acc)
        if diagonal:
            a = jnp.where(strict, a, 0.0)
        return jnp.dot(a.astype(BF16), vj, preferred_element_type=F32)

    def two_blocks(group, diagonal):
        scored = [(scores(qb, j_a), scores(qb, j_b)) for qb, j_a, j_b, _, _, _ in group]
        summed = [(suffix(sa[1], diagonal), suffix(sb[1], False)) for sa, sb in scored]
        out = []
        for (qb, j_a, j_b, use_b, acc, o), (sa, sb), ((aft_a, tot_a), (aft_b, tot_b)) in zip(group, scored, summed):
            acc_a = acc + tot_a
            o_a = o + weigh(j_a, sa[0], aft_a, acc, diagonal)
            o_b = o_a + weigh(j_b, sb[0], aft_b, acc_a, False)
            out.append((jnp.where(use_b, acc_a + tot_b, acc_a), jnp.where(use_b, o_b, o_a)))
        return out

    group = []
    for r in range(SB_SUBBLOCKS):
        i = pl.program_id(2) * SB_SUBBLOCKS + r
        qb = q_ref[0, r * t:(r + 1) * t, :]
        group.append((qb, i, jnp.maximum(i - 1, 0), i >= 1,
                      jnp.zeros((t, 1), F32), jnp.zeros((t, SB_HEAD_DIM), F32)))
    started = [(pl.program_id(2) * SB_SUBBLOCKS + r, g[0], acc, o)
               for r, (g, (acc, o)) in enumerate(zip(group, two_blocks(group, True)))]

    for r, (i, qb, acc, o) in enumerate(started):
        def unfinished(carry, i=i):
            step, acc, _ = carry
            return jnp.logical_and(step < i // 2, jnp.max(acc) > SB_EXHAUSTED_LOG2)

        def body(carry, i=i, qb=qb):
            step, acc, o = carry
            j_a = i - 2 - 2 * step
            (acc, o), = two_blocks([(qb, j_a, jnp.maximum(j_a - 1, 0), j_a >= 1, acc, o)], False)
            return step + 1, acc, o

        _, _, o = lax.while_loop(unfinished, body, (jnp.int32(0), acc, o))
        o_ref[0, r * t:(r + 1) * t, :] = o


def _sb_mixer(qkv, batch, seq, tile=256):
    rows = SB_SUBBLOCKS * tile
    assert seq % rows == 0
    nq = seq // rows
    qkv3 = qkv.reshape(batch, seq, 3 * MAIN_WIDTH)
    return pl.pallas_call(
        functools.partial(_sb_kernel, tile=tile),
        grid=(batch, SB_HEADS, nq),
        in_specs=[pl.BlockSpec((1, rows, SB_HEAD_DIM), lambda b, h, i: (b, i, h)),
                  pl.BlockSpec((1, seq, SB_HEAD_DIM), lambda b, h, i: (b, 0, SB_HEADS + h)),
                  pl.BlockSpec((1, seq, SB_HEAD_DIM), lambda b, h, i: (b, 0, 2 * SB_HEADS + h))],
        out_specs=pl.BlockSpec((1, rows, SB_HEAD_DIM), lambda b, h, i: (b, i, h)),
        out_shape=jax.ShapeDtypeStruct((batch, seq, MAIN_WIDTH), F32),
        compiler_params=_cparams(("parallel", "parallel", "arbitrary")),
        name="stick_breaking",
    )(qkv3, qkv3, qkv3)


def _mem_kernel(qm_ref, kv_ref, o_ref):
    scale = MEM_HEAD_DIM ** -0.5
    for h in range(MEM_HEADS):
        hs = slice(h * MEM_HEAD_DIM, (h + 1) * MEM_HEAD_DIM)
        kh = kv_ref[0, :, hs]
        vh = kv_ref[0, :, MEM_WIDTH + h * MEM_HEAD_DIM:MEM_WIDTH + (h + 1) * MEM_HEAD_DIM]
        s = lax.dot_general(qm_ref[:, hs], kh, (((1,), (1,)), ((), ())),
                            preferred_element_type=F32) * scale
        e = jnp.exp(s - jnp.max(s, axis=-1, keepdims=True))
        p = e * (1.0 / jnp.sum(e, axis=-1, keepdims=True))
        o_ref[:, hs] = jnp.dot(p.astype(BF16), vh, preferred_element_type=F32)


def _mem_attention(q_mem, kv, seq, tm=512):
    m = q_mem.shape[0]
    tiles_per_batch = seq // tm
    return pl.pallas_call(
        _mem_kernel,
        grid=(m // tm,),
        in_specs=[pl.BlockSpec((tm, MEM_WIDTH), lambda i: (i, 0)),
                  pl.BlockSpec((1, MEM_TOKENS, 2 * MEM_WIDTH), lambda i: (i // tiles_per_batch, 0, 0))],
        out_specs=pl.BlockSpec((tm, MEM_WIDTH), lambda i: (i, 0)),
        out_shape=jax.ShapeDtypeStruct((m, MEM_WIDTH), F32),
        compiler_params=_cparams(("parallel",)),
        name="memory_attention",
    )(q_mem, kv)


def kernel(x, mem, mem_norm, ssd_norm, ssd_w_in, ssd_conv_w, ssd_conv_b, ssd_dt_bias, ssd_a_log, ssd_d, ssd_mem_kv, ssd_out_norm, ssd_w_out, sb_norm, sb_w_in, sb_mem_kv, sb_out_norm, sb_w_out, final_norm):
    batch, seq, d = x.shape
    m = batch * seq
    depth = ssd_w_in.shape[0] + sb_w_in.shape[0]
    xf = x.reshape(m, d)
    mem_n = _rmsnorm(mem.reshape(batch * MEM_TOKENS, d), mem_norm, BF16)
    in_norm = lambda layer: (ssd_norm if layer % 2 == 0 else sb_norm)[layer // 2]

    xg, inv = _prep(xf, in_norm(0))
    for layer in range(depth):
        j = layer // 2
        if layer % 2 == 0:
            w_in, w_kv, out_norm, w_out = ssd_w_in, ssd_mem_kv, ssd_out_norm, ssd_w_out
            c0, c1, c2 = SSD_CONV_DIM, SSD_CONV_DIM + SSD_HEADS, SSD_CONV_DIM + SSD_HEADS + MIX_WIDTH
            w_dt = jnp.pad(w_in[j, :, c0:c1], ((0, 0), (0, HEAD_PAD - SSD_HEADS)))[None]
            w_z, lz, z0 = w_in[j:j + 1, :, c1:c2], 0, 0
            w_q, lq, q0 = w_in[j:j + 1, :, c2:], 0, 0
            xbc = _project(xg, w_in, j, 0, c0, F32, inv=inv)
            dt_raw = _project(xg, w_dt, 0, 0, HEAD_PAD, F32, inv=inv)
            main = _ssd_mixer(xbc, dt_raw, ssd_conv_w[j], ssd_conv_b[j], ssd_dt_bias[j],
                              ssd_a_log[j], ssd_d[j], batch, seq)
        else:
            w_in, w_kv, out_norm, w_out = sb_w_in, sb_mem_kv, sb_out_norm, sb_w_out
            c0, c1 = 3 * MAIN_WIDTH, 3 * MAIN_WIDTH + MIX_WIDTH
            w_z, lz, z0 = w_in, j, c0
            w_q, lq, q0 = w_in, j, c1
            qkv = _project(xg, w_in, j, 0, c0, BF16, inv=inv, lead_scale=(MAIN_WIDTH, SB_QUERY_SCALE))
            main = _sb_mixer(qkv, batch, seq)
        q_mem = _project(xg, w_q, lq, q0, MEM_WIDTH, BF16, inv=inv)
        kv = _project(mem_n, w_kv, j, 0, 2 * MEM_WIDTH, BF16).reshape(batch, MEM_TOKENS, 2 * MEM_WIDTH)
        mem_out = _mem_attention(q_mem, kv, seq)
        yg, ssq = _project(xg, w_z, lz, z0, MIX_WIDTH, BF16, inv=inv,
                           gate=(main.reshape(m, MAIN_WIDTH), mem_out, out_norm[j]))
        g_next = in_norm(layer + 1) if layer + 1 < depth else final_norm
        xf, xg, inv = _out_project(yg, ssq, w_out[j].astype(BF16), xf, g_next)

    return _rmsnorm(xf, final_norm, F32).reshape(batch, seq, d)
```

```python
import functools

import jax
import jax.numpy as jnp
from jax import lax
from jax.experimental import pallas as pl
from jax.experimental.pallas import tpu as pltpu

D_MODEL = 2048
MEM_TOKENS = 256
MIX_WIDTH = 2 * D_MODEL
MEM_HEADS = 4
MEM_WIDTH = MIX_WIDTH // 4
MEM_HEAD_DIM = MEM_WIDTH // MEM_HEADS
MAIN_WIDTH = MIX_WIDTH - MEM_WIDTH
SSD_HEAD_DIM = 64
SSD_HEADS = MAIN_WIDTH // SSD_HEAD_DIM
SSD_GROUPS = 8
SSD_HEADS_PER_GROUP = SSD_HEADS // SSD_GROUPS
SSD_STATE = 128
SSD_CONV = 4
SSD_CHUNK = 128
SSD_BC_WIDTH = SSD_GROUPS * SSD_STATE
SSD_CONV_DIM = MAIN_WIDTH + 2 * SSD_BC_WIDTH
SB_HEAD_DIM = 128
SB_HEADS = MAIN_WIDTH // SB_HEAD_DIM
EPS = 1e-6
LOG2E = 1.4426950408889634
SB_QUERY_SCALE = SB_HEAD_DIM ** -0.5 * LOG2E
SB_SUBBLOCKS = 16
SB_EXHAUSTED_LOG2 = -160.0

LANES = 128
SUBLANES = 8
HEAD_PAD = LANES
VMEM_LIMIT = 56 * 1024 * 1024

BF16 = jnp.bfloat16
F32 = jnp.float32


def _cparams(semantics):
    return pltpu.CompilerParams(dimension_semantics=semantics, vmem_limit_bytes=VMEM_LIMIT)


def _rms_kernel(x_ref, g_ref, o_ref):
    x = x_ref[...]
    ms = jnp.mean(x * x, axis=-1, keepdims=True)
    o_ref[...] = (x * lax.rsqrt(ms + EPS) * g_ref[...]).astype(o_ref.dtype)


def _rmsnorm(x, g, out_dtype, tm=512):
    m, d = x.shape
    tm = min(tm, m)
    return pl.pallas_call(
        _rms_kernel,
        grid=(m // tm,),
        in_specs=[pl.BlockSpec((tm, d), lambda i: (i, 0)),
                  pl.BlockSpec((1, d), lambda i: (0, 0))],
        out_specs=pl.BlockSpec((tm, d), lambda i: (i, 0)),
        out_shape=jax.ShapeDtypeStruct((m, d), out_dtype),
        compiler_params=_cparams(("parallel",)),
        name="rmsnorm",
    )(x, g.reshape(1, d))


def _inv_rms(sum_sq, width):
    return lax.rsqrt(sum_sq * (1.0 / width) + EPS)


def _prep_kernel(x_ref, g_ref, xg_ref, inv_ref):
    x = x_ref[...]
    xg_ref[...] = (x * g_ref[...]).astype(BF16)
    inv = _inv_rms(jnp.sum(x * x, axis=-1, keepdims=True), x.shape[-1])
    inv_ref[...] = jnp.broadcast_to(inv, inv_ref.shape)


def _prep(x, g, tm=512):
    m, d = x.shape
    return pl.pallas_call(
        _prep_kernel,
        grid=(m // tm,),
        in_specs=[pl.BlockSpec((tm, d), lambda i: (i, 0)),
                  pl.BlockSpec((1, d), lambda i: (0, 0))],
        out_specs=[pl.BlockSpec((tm, d), lambda i: (i, 0)),
                   pl.BlockSpec((tm, LANES), lambda i: (i, 0))],
        out_shape=[jax.ShapeDtypeStruct((m, d), BF16), jax.ShapeDtypeStruct((m, LANES), F32)],
        compiler_params=_cparams(("parallel",)),
        name="norm_prep",
    )(x, g.reshape(1, d))


def _cast_weight_tile(w_ref, wb_ref):
    @pl.when(pl.program_id(1) == 0)
    def _():
        wb_ref[...] = w_ref[...].astype(BF16)


def _proj_kernel(x_ref, w_ref, o_ref, wb_ref):
    _cast_weight_tile(w_ref, wb_ref)
    o_ref[...] = jnp.dot(x_ref[...], wb_ref[...], preferred_element_type=F32).astype(o_ref.dtype)


def _proj_inv_kernel(x_ref, inv_ref, w_ref, o_ref, wb_ref, *, scaled_tiles, tile_scale):
    _cast_weight_tile(w_ref, wb_ref)
    acc = jnp.dot(x_ref[...], wb_ref[...], preferred_element_type=F32)
    row_scale = inv_ref[:, 0:1]
    if scaled_tiles:
        row_scale = row_scale * jnp.where(pl.program_id(0) < scaled_tiles, tile_scale, 1.0)
    o_ref[...] = (acc * row_scale).astype(o_ref.dtype)


def _gate_kernel(x_ref, inv_ref, w_ref, main_ref, mem_ref, g_ref, yg_ref, ssq_ref, wb_ref, *, main_tiles):
    _cast_weight_tile(w_ref, wb_ref)
    z = jnp.dot(x_ref[...], wb_ref[...], preferred_element_type=F32) * inv_ref[:, 0:1]
    other = jnp.where(pl.program_id(0) < main_tiles, main_ref[...], mem_ref[...])
    y = other * _silu(z)
    yg_ref[...] = (y * g_ref[...]).astype(BF16)
    ssq_ref[...] = jnp.broadcast_to(jnp.sum(y * y, axis=-1, keepdims=True), ssq_ref.shape)


def _project(x, w, layer, col0, n, out_dtype, inv=None, gate=None, lead_scale=None, tm=1024, tn=1024):
    m, k = x.shape
    tm = min(tm, m)
    tn = min(tn, n)
    assert col0 % tn == 0 and n % tn == 0 and m % tm == 0
    j0 = col0 // tn
    x_spec = pl.BlockSpec((tm, k), lambda j, i: (i, 0))
    inv_spec = pl.BlockSpec((tm, LANES), lambda j, i: (i, 0))
    w_spec = pl.BlockSpec((None, k, tn), lambda j, i: (layer, 0, j0 + j))
    tile_spec = pl.BlockSpec((tm, tn), lambda j, i: (i, j))
    common = dict(grid=(n // tn, m // tm), scratch_shapes=[pltpu.VMEM((k, tn), BF16)],
                  compiler_params=_cparams(("parallel", "arbitrary")))
    if gate is not None:
        main, mem, g = gate
        assert main.shape[1] % tn == 0 and mem.shape[1] == tn and main.shape[1] + tn == n
        main_tiles = main.shape[1] // tn
        main_spec = pl.BlockSpec((tm, tn), lambda j, i: (jnp.where(j < main_tiles, i, 0),
                                                         jnp.minimum(j, main_tiles - 1)))
        mem_spec = pl.BlockSpec((tm, tn), lambda j, i: (jnp.where(j < main_tiles, 0, i), 0))
        return pl.pallas_call(
            functools.partial(_gate_kernel, main_tiles=main_tiles),
            in_specs=[x_spec, inv_spec, w_spec, main_spec, mem_spec,
                      pl.BlockSpec((1, tn), lambda j, i: (0, j))],
            out_specs=[tile_spec, pl.BlockSpec((None, tm, LANES), lambda j, i: (j, i, 0))],
            out_shape=[jax.ShapeDtypeStruct((m, n), BF16),
                       jax.ShapeDtypeStruct((n // tn, m, LANES), F32)],
            name="gate_projection", **common,
        )(x, inv, w, main, mem, g.reshape(1, n))
    if inv is not None:
        width, c = lead_scale if lead_scale is not None else (0, 1.0)
        assert width % tn == 0
        body = functools.partial(_proj_inv_kernel, scaled_tiles=width // tn, tile_scale=c)
        return pl.pallas_call(
            body, in_specs=[x_spec, inv_spec, w_spec], out_specs=tile_spec,
            out_shape=jax.ShapeDtypeStruct((m, n), out_dtype), name="projection", **common,
        )(x, inv, w)
    return pl.pallas_call(
        _proj_kernel, in_specs=[x_spec, w_spec], out_specs=tile_spec,
        out_shape=jax.ShapeDtypeStruct((m, n), out_dtype), name="projection_plain", **common,
    )(x, w)


def _out_kernel(yg_ref, ssq_ref, w_ref, res_ref, gn_ref, x_ref, xg_ref, invn_ref, invy_ref, acc_ref):
    j = pl.program_id(1)

    @pl.when(j == 0)
    def _():
        total = ssq_ref[0, :, 0:1]
        for p in range(1, ssq_ref.shape[0]):
            total = total + ssq_ref[p, :, 0:1]
        invy_ref[...] = _inv_rms(total, MIX_WIDTH)
        acc_ref[...] = jnp.zeros_like(acc_ref)

    x_new = res_ref[...] + jnp.dot(yg_ref[...], w_ref[...], preferred_element_type=F32) * invy_ref[...]
    x_ref[...] = x_new
    xg_ref[...] = (x_new * gn_ref[...]).astype(BF16)
    acc_ref[...] += jnp.sum(x_new * x_new, axis=-1, keepdims=True)

    @pl.when(j == pl.num_programs(1) - 1)
    def _():
        invn_ref[...] = jnp.broadcast_to(_inv_rms(acc_ref[...], D_MODEL), invn_ref.shape)


def _out_project(yg, ssq, w, res, g_next, tm=1024, tn=512):
    m, k = yg.shape
    n = w.shape[1]
    parts = ssq.shape[0]
    return pl.pallas_call(
        _out_kernel,
        grid=(m // tm, n // tn),
        in_specs=[pl.BlockSpec((tm, k), lambda i, j: (i, 0)),
                  pl.BlockSpec((parts, tm, LANES), lambda i, j: (0, i, 0)),
                  pl.BlockSpec((k, tn), lambda i, j: (0, j)),
                  pl.BlockSpec((tm, tn), lambda i, j: (i, j)),
                  pl.BlockSpec((1, tn), lambda i, j: (0, j))],
        out_specs=[pl.BlockSpec((tm, tn), lambda i, j: (i, j)),
                   pl.BlockSpec((tm, tn), lambda i, j: (i, j)),
                   pl.BlockSpec((tm, LANES), lambda i, j: (i, 0))],
        out_shape=[jax.ShapeDtypeStruct((m, n), F32), jax.ShapeDtypeStruct((m, n), BF16),
                   jax.ShapeDtypeStruct((m, LANES), F32)],
        scratch_shapes=[pltpu.VMEM((tm, 1), F32), pltpu.VMEM((tm, 1), F32)],
        compiler_params=_cparams(("parallel", "arbitrary")),
        name="out_projection",
    )(yg, ssq, w, res, g_next.reshape(1, n))


def _split3(x):
    p0 = x.astype(BF16)
    r1 = x - p0.astype(F32)
    p1 = r1.astype(BF16)
    p2 = (r1 - p1.astype(F32)).astype(BF16)
    return p0, p1, p2


def _dot_f32_by_01(x, ones_rhs):
    p0, p1, p2 = _split3(x)
    d = functools.partial(jnp.dot, preferred_element_type=F32)
    return d(p0, ones_rhs) + (d(p1, ones_rhs) + d(p2, ones_rhs))


def _dot_01_by_f32(ones_lhs, x):
    p0, p1, p2 = _split3(x)
    d = functools.partial(jnp.dot, preferred_element_type=F32)
    return d(ones_lhs, p0) + (d(ones_lhs, p1) + d(ones_lhs, p2))


def _softplus(x):
    return jnp.maximum(x, 0.0) + jnp.log(1.0 + jnp.exp(-jnp.abs(x)))


def _silu(x):
    return x / (1.0 + jnp.exp(-x))


def _ssd_kernel(xbc_ref, dt_ref, cw_ref, cb_ref, dtb_ref, alog_ref, expand_ref, dskip_ref,
                y_ref, ext_ref, carry_ref, xc_ref, ex_ref, st_ref):
    q = SSD_CHUNK
    hp = SSD_HEAD_DIM
    gw = SSD_HEADS_PER_GROUP * hp

    @pl.when(pl.program_id(1) == 0)
    def _():
        carry_ref[...] = jnp.zeros_like(carry_ref)
        st_ref[...] = jnp.zeros_like(st_ref)

    u = xbc_ref[0]
    shifted = None
    for tap in range(SSD_CONV - 1):
        part = cw_ref[tap:tap + 1, :] * u
        if shifted is not None:
            part = part + shifted
        ext_ref[...] = pltpu.roll(part, 1, axis=0)
        last_row = ext_ref[0:1, :]
        ext_ref[0:1, :] = carry_ref[tap:tap + 1, :]
        carry_ref[tap:tap + 1, :] = last_row
        shifted = ext_ref[...]
    conv = cb_ref[...] + cw_ref[SSD_CONV - 1:SSD_CONV, :] * u + shifted
    xc_ref[...] = _silu(conv)

    dt = _softplus(dt_ref[0] + dtb_ref[...])
    d_a = dt * (-jnp.exp(alog_ref[...]))
    row = lax.broadcasted_iota(jnp.int32, (q, q), 0)
    col = lax.broadcasted_iota(jnp.int32, (q, q), 1)
    causal = row >= col
    cum = _dot_01_by_f32(causal.astype(BF16), d_a)
    cum_t = cum.T
    cum_end = cum[q - 1:q, :]
    to_end = jnp.exp(cum_end - cum) * dt
    ecum = jnp.exp(cum)

    expand = expand_ref[...]
    ex_ref[0:2 * q, :] = jnp.dot(jnp.concatenate([dt, to_end], axis=0).astype(BF16), expand,
                                 preferred_element_type=F32)
    ecum_hi = ecum.astype(BF16)
    ecum_lo = (ecum - ecum_hi.astype(F32)).astype(BF16)
    ex_ref[2 * q:3 * q, :] = (jnp.dot(ecum_hi, expand, preferred_element_type=F32)
                              + jnp.dot(ecum_lo, expand, preferred_element_type=F32))

    lane = lax.broadcasted_iota(jnp.int32, (q, LANES), 1)
    low_half = lane < hp

    for g in range(SSD_GROUPS):
        bg = xc_ref[:, MAIN_WIDTH + g * SSD_STATE:MAIN_WIDTH + (g + 1) * SSD_STATE]
        cg = xc_ref[:, MAIN_WIDTH + SSD_BC_WIDTH + g * SSD_STATE:
                    MAIN_WIDTH + SSD_BC_WIDTH + (g + 1) * SSD_STATE]
        cg16 = cg.astype(BF16)
        cb = lax.dot_general(cg16, bg.astype(BF16), (((1,), (1,)), ((), ())),
                             preferred_element_type=F32)
        gs = slice(g * gw, (g + 1) * gw)
        y_off = jnp.dot(cg16, st_ref[:, gs].astype(BF16), preferred_element_type=F32)

        for pair in range(SSD_HEADS_PER_GROUP // 2):
            ws = []
            for r in (2 * pair, 2 * pair + 1):
                h = g * SSD_HEADS_PER_GROUP + r
                seg = cum[:, h:h + 1] - cum_t[h:h + 1, :]
                ws.append(cb * jnp.exp(jnp.where(causal, seg, -jnp.inf)))
            w2 = jnp.concatenate(ws, axis=1).astype(BF16)
            ps = slice(g * gw + pair * LANES, g * gw + (pair + 1) * LANES)
            xs = xc_ref[:, ps]
            xdt = xs * ex_ref[0:q, ps]
            rhs = jnp.concatenate([jnp.where(low_half, xdt, 0.0),
                                   jnp.where(low_half, 0.0, xdt)], axis=0).astype(BF16)
            y_diag = jnp.dot(w2, rhs, preferred_element_type=F32)
            yo = y_off[:, pair * LANES:(pair + 1) * LANES]
            y_ref[0, :, ps] = y_diag + ex_ref[2 * q:3 * q, ps] * yo + xs * dskip_ref[:, ps]

        xte = (xc_ref[:, gs] * ex_ref[q:2 * q, gs]).astype(BF16)
        upd = jnp.dot(bg.T.astype(BF16), xte, preferred_element_type=F32)
        st_ref[:, gs] = st_ref[:, gs] * ex_ref[3 * q - 1:3 * q, gs] + upd


def _ssd_mixer(xbc, dt_raw, conv_w, conv_b, dt_bias, a_log, d_skip, batch, seq):
    q = SSD_CHUNK
    nchunks = seq // q
    pad = HEAD_PAD - SSD_HEADS
    dtb = jnp.pad(dt_bias, (0, pad)).reshape(1, HEAD_PAD)
    alog = jnp.pad(a_log, (0, pad)).reshape(1, HEAD_PAD)
    head_of_lane = jnp.arange(MAIN_WIDTH) // SSD_HEAD_DIM
    expand = (jnp.arange(HEAD_PAD)[:, None] == head_of_lane[None, :]).astype(BF16)
    dskip = jnp.repeat(d_skip, SSD_HEAD_DIM).reshape(1, MAIN_WIDTH)
    const = lambda shape: pl.BlockSpec(shape, lambda b, c: (0,) * len(shape))
    return pl.pallas_call(
        _ssd_kernel,
        grid=(batch, nchunks),
        in_specs=[pl.BlockSpec((1, q, SSD_CONV_DIM), lambda b, c: (b, c, 0)),
                  pl.BlockSpec((1, q, HEAD_PAD), lambda b, c: (b, c, 0)),
                  const((SSD_CONV, SSD_CONV_DIM)),
                  const((1, SSD_CONV_DIM)),
                  const((1, HEAD_PAD)),
                  const((1, HEAD_PAD)),
                  const((HEAD_PAD, MAIN_WIDTH)),
                  const((1, MAIN_WIDTH))],
        out_specs=pl.BlockSpec((1, q, MAIN_WIDTH), lambda b, c: (b, c, 0)),
        out_shape=jax.ShapeDtypeStruct((batch, seq, MAIN_WIDTH), F32),
        scratch_shapes=[pltpu.VMEM((q, SSD_CONV_DIM), F32),
                        pltpu.VMEM((SUBLANES, SSD_CONV_DIM), F32),
                        pltpu.VMEM((q, SSD_CONV_DIM), F32),
                        pltpu.VMEM((3 * q, MAIN_WIDTH), F32),
                        pltpu.VMEM((SSD_STATE, MAIN_WIDTH), F32)],
        compiler_params=_cparams(("parallel", "arbitrary")),
        name="ssd_mixer",
    )(xbc.reshape(batch, seq, SSD_CONV_DIM), dt_raw.reshape(batch, seq, HEAD_PAD),
      conv_w, conv_b.reshape(1, SSD_CONV_DIM), dtb, alog, expand, dskip)


def _sb_kernel(q_ref, k_ref, v_ref, o_ref, *, tile):
    t = tile
    row = lax.broadcasted_iota(jnp.int32, (t, t), 0)
    col = lax.broadcasted_iota(jnp.int32, (t, t), 1)
    strict = row > col
    later = strict.astype(BF16)

    def scores(qb, j):
        kj = k_ref[0, pl.ds(pl.multiple_of(j * t, t), t), :]
        z2 = lax.dot_general(qb, kj, (((1,), (1,)), ((), ())), preferred_element_type=F32)
        neg_abs = lax.bitcast_convert_type(
            lax.bitcast_convert_type(z2, jnp.uint32) | jnp.uint32(0x80000000), F32)
        log_sig = jnp.minimum(z2, 0.0) - jnp.log2(1.0 + jnp.exp2(neg_abs))
        return log_sig, log_sig - z2

    def suffix(log_keep, diagonal):
        if diagonal:
            log_keep = jnp.where(strict, log_keep, 0.0)
        after = jnp.dot(log_keep.astype(BF16), later, preferred_element_type=F32)
        return after, jnp.sum(log_keep, axis=1, keepdims=True)

    def weigh(j, log_sig, after, acc, diagonal):
        vj = v_ref[0, pl.ds(pl.multiple_of(j * t, t), t), :]
        a = jnp.exp2(log_sig + after + acc)
        if diagonal:
            a = jnp.where(strict, a, 0.0)
        return jnp.dot(a.astype(BF16), vj, preferred_element_type=F32)

    def two_blocks(group, diagonal):
        scored = [(scores(qb, j_a), scores(qb, j_b)) for qb, j_a, j_b, _, _, _ in group]
        summed = [(suffix(sa[1], diagonal), suffix(sb[1], False)) for sa, sb in scored]
        out = []
        for (qb, j_a, j_b, use_b, acc, o), (sa, sb), ((aft_a, tot_a), (aft_b, tot_b)) in zip(group, scored, summed):
            acc_a = acc + tot_a
            o_a = o + weigh(j_a, sa[0], aft_a, acc, diagonal)
            o_b = o_a + weigh(j_b, sb[0], aft_b, acc_a, False)
            out.append((jnp.where(use_b, acc_a + tot_b, acc_a), jnp.where(use_b, o_b, o_a)))
        return out

    group = []
    for r in range(SB_SUBBLOCKS):
        i = pl.program_id(2) * SB_SUBBLOCKS + r
        qb = q_ref[0, r * t:(r + 1) * t, :]
        group.append((qb, i, jnp.maximum(i - 1, 0), i >= 1,
                      jnp.zeros((t, 1), F32), jnp.zeros((t, SB_HEAD_DIM), F32)))
    started = [(pl.program_id(2) * SB_SUBBLOCKS + r, g[0], acc, o)
               for r, (g, (acc, o)) in enumerate(zip(group, two_blocks(group, True)))]

    for r, (i, qb, acc, o) in enumerate(started):
        def unfinished(carry, i=i):
            step, acc, _ = carry
            return jnp.logical_and(step < i // 2, jnp.max(acc) > SB_EXHAUSTED_LOG2)

        def body(carry, i=i, qb=qb):
            step, acc, o = carry
            j_a = i - 2 - 2 * step
            (acc, o), = two_blocks([(qb, j_a, jnp.maximum(j_a - 1, 0), j_a >= 1, acc, o)], False)
            return step + 1, acc, o

        _, _, o = lax.while_loop(unfinished, body, (jnp.int32(0), acc, o))
        o_ref[0, r * t:(r + 1) * t, :] = o


def _sb_mixer(qkv, batch, seq, tile=256):
    rows = SB_SUBBLOCKS * tile
    assert seq % rows == 0
    nq = seq // rows
    qkv3 = qkv.reshape(batch, seq, 3 * MAIN_WIDTH)
    return pl.pallas_call(
        functools.partial(_sb_kernel, tile=tile),
        grid=(batch, SB_HEADS, nq),
        in_specs=[pl.BlockSpec((1, rows, SB_HEAD_DIM), lambda b, h, i: (b, i, h)),
                  pl.BlockSpec((1, seq, SB_HEAD_DIM), lambda b, h, i: (b, 0, SB_HEADS + h)),
                  pl.BlockSpec((1, seq, SB_HEAD_DIM), lambda b, h, i: (b, 0, 2 * SB_HEADS + h))],
        out_specs=pl.BlockSpec((1, rows, SB_HEAD_DIM), lambda b, h, i: (b, i, h)),
        out_shape=jax.ShapeDtypeStruct((batch, seq, MAIN_WIDTH), F32),
        compiler_params=_cparams(("parallel", "parallel", "arbitrary")),
        name="stick_breaking",
    )(qkv3, qkv3, qkv3)


def _mem_kernel(x_ref, inv_ref, w_ref, kv_ref, o_ref, wb_ref):
    @pl.when(pl.program_id(0) == 0)
    def _():
        wb_ref[...] = w_ref[...].astype(BF16)

    q = (jnp.dot(x_ref[...], wb_ref[...], preferred_element_type=F32) * inv_ref[:, 0:1]).astype(BF16)
    scale = MEM_HEAD_DIM ** -0.5
    for h in range(MEM_HEADS):
        hs = slice(h * MEM_HEAD_DIM, (h + 1) * MEM_HEAD_DIM)
        kh = kv_ref[0, :, hs]
        vh = kv_ref[0, :, MEM_WIDTH + h * MEM_HEAD_DIM:MEM_WIDTH + (h + 1) * MEM_HEAD_DIM]
        s = lax.dot_general(q[:, hs], kh, (((1,), (1,)), ((), ())),
                            preferred_element_type=F32) * scale
        e = jnp.exp(s - jnp.max(s, axis=-1, keepdims=True))
        p = e * (1.0 / jnp.sum(e, axis=-1, keepdims=True))
        o_ref[:, hs] = jnp.dot(p.astype(BF16), vh, preferred_element_type=F32)


def _mem_attention(x, inv, w, layer, col0, kv, seq, tm=1024):
    m, k = x.shape
    assert col0 % MEM_WIDTH == 0 and m % tm == 0 and seq % tm == 0
    tiles_per_batch = seq // tm
    return pl.pallas_call(
        _mem_kernel,
        grid=(m // tm,),
        in_specs=[pl.BlockSpec((tm, k), lambda i: (i, 0)),
                  pl.BlockSpec((tm, LANES), lambda i: (i, 0)),
                  pl.BlockSpec((None, k, MEM_WIDTH), lambda i: (layer, 0, col0 // MEM_WIDTH)),
                  pl.BlockSpec((1, MEM_TOKENS, 2 * MEM_WIDTH), lambda i: (i // tiles_per_batch, 0, 0))],
        out_specs=pl.BlockSpec((tm, MEM_WIDTH), lambda i: (i, 0)),
        out_shape=jax.ShapeDtypeStruct((m, MEM_WIDTH), F32),
        scratch_shapes=[pltpu.VMEM((k, MEM_WIDTH), BF16)],
        compiler_params=_cparams(("arbitrary",)),
        name="memory_attention",
    )(x, inv, w, kv)


def kernel(x, mem, mem_norm, ssd_norm, ssd_w_in, ssd_conv_w, ssd_conv_b, ssd_dt_bias, ssd_a_log, ssd_d, ssd_mem_kv, ssd_out_norm, ssd_w_out, sb_norm, sb_w_in, sb_mem_kv, sb_out_norm, sb_w_out, final_norm):
    batch, seq, d = x.shape
    m = batch * seq
    depth = ssd_w_in.shape[0] + sb_w_in.shape[0]
    xf = x.reshape(m, d)
    mem_n = _rmsnorm(mem.reshape(batch * MEM_TOKENS, d), mem_norm, BF16)
    in_norm = lambda layer: (ssd_norm if layer % 2 == 0 else sb_norm)[layer // 2]

    xg, inv = _prep(xf, in_norm(0))
    for layer in range(depth):
        j = layer // 2
        if layer % 2 == 0:
            w_in, w_kv, out_norm, w_out = ssd_w_in, ssd_mem_kv, ssd_out_norm, ssd_w_out
            c0, c1, c2 = SSD_CONV_DIM, SSD_CONV_DIM + SSD_HEADS, SSD_CONV_DIM + SSD_HEADS + MIX_WIDTH
            w_dt = jnp.pad(w_in[j, :, c0:c1], ((0, 0), (0, HEAD_PAD - SSD_HEADS)))[None]
            w_z, lz, z0 = w_in[j:j + 1, :, c1:c2], 0, 0
            w_q, lq, q0 = w_in[j:j + 1, :, c2:], 0, 0
            xbc = _project(xg, w_in, j, 0, c0, F32, inv=inv)
            dt_raw = _project(xg, w_dt, 0, 0, HEAD_PAD, F32, inv=inv)
            main = _ssd_mixer(xbc, dt_raw, ssd_conv_w[j], ssd_conv_b[j], ssd_dt_bias[j],
                              ssd_a_log[j], ssd_d[j], batch, seq)
        else:
            w_in, w_kv, out_norm, w_out = sb_w_in, sb_mem_kv, sb_out_norm, sb_w_out
            c0, c1 = 3 * MAIN_WIDTH, 3 * MAIN_WIDTH + MIX_WIDTH
            w_z, lz, z0 = w_in, j, c0
            w_q, lq, q0 = w_in, j, c1
            qkv = _project(xg, w_in, j, 0, c0, BF16, inv=inv, lead_scale=(MAIN_WIDTH, SB_QUERY_SCALE))
            main = _sb_mixer(qkv, batch, seq)
        kv = _project(mem_n, w_kv, j, 0, 2 * MEM_WIDTH, BF16).reshape(batch, MEM_TOKENS, 2 * MEM_WIDTH)
        mem_out = _mem_attention(xg, inv, w_q, lq, q0, kv, seq)
        yg, ssq = _project(xg, w_z, lz, z0, MIX_WIDTH, BF16, inv=inv,
                           gate=(main.reshape(m, MAIN_WIDTH), mem_out, out_norm[j]))
        g_next = in_norm(layer + 1) if layer + 1 < depth else final_norm
        xf, xg, inv = _out_project(yg, ssq, w_out[j].astype(BF16), xf, g_next)

    return _rmsnorm(xf, final_norm, F32).reshape(batch, seq, d)
```
